```python
import math
import jax
import jax.numpy as jnp
from jax import lax
import numpy as np

D_MODEL = 2048
BATCH = 1
SEQ = 8192
DEPTH = 2
DEC_BATCH = 128
DEC_SEQ = 8
PAST_LEN = 2048
PAGE_SIZE = 128

HEAD_DIM = 128
MIX_WIDTH = D_MODEL // 2
N_HEADS = MIX_WIDTH // HEAD_DIM
N_BRANCH = 3
CONV_W = 4
CHUNK = 64
IDX_HEADS = 16
IDX_DIM = 64
TOPK_MAX = 256
Q_BLOCK = 128
NUM_BUCKETS = 32
MAX_DISTANCE = 128
D_FF = ((8 * D_MODEL // 3 + 127) // 128) * 128
FORGET_BIAS = 3.0
EPS = 1e-6

SEG_WIDTHS = (
    3 * MIX_WIDTH, MIX_WIDTH, N_HEADS, N_HEADS,
    MIX_WIDTH, MIX_WIDTH, MIX_WIDTH, MIX_WIDTH, N_HEADS, N_HEADS,
    MIX_WIDTH, MIX_WIDTH, MIX_WIDTH,
    IDX_HEADS * IDX_DIM, IDX_HEADS, IDX_DIM,
    N_BRANCH * D_MODEL,
)
N_IN = sum(SEG_WIDTHS)

kernel_name = 'hybrid_gdn_mlstm_dsa_decoder_step'


def _split_in(proj):
    offs = []
    acc = 0
    for w in SEG_WIDTHS[:-1]:
        acc += w
        offs.append(acc)
    return jnp.split(proj, offs, axis=-1)


def rmsnorm(x, g):
    xf = x.astype(jnp.float32)
    y = xf * lax.rsqrt(jnp.mean(xf * xf, axis=-1, keepdims=True) + EPS)
    return (y * g.astype(jnp.float32)).astype(x.dtype)


def layernorm(x, g, b):
    xf = x.astype(jnp.float32)
    mu = jnp.mean(xf, axis=-1, keepdims=True)
    var = jnp.mean(jnp.square(xf - mu), axis=-1, keepdims=True)
    y = (xf - mu) * lax.rsqrt(var + EPS) * g.astype(jnp.float32) + b.astype(jnp.float32)
    return y.astype(x.dtype)


def l2norm(x):
    return x * lax.rsqrt(jnp.sum(x * x, axis=-1, keepdims=True) + EPS)


def swiglu(x, w13, w2):
    gu = x @ w13
    g, u = jnp.split(gu, 2, axis=-1)
    return (jax.nn.silu(g) * u) @ w2


def causal_conv(u, state, w):
    t = u.shape[1]
    xp = jnp.concatenate([state.astype(u.dtype), u], axis=1)
    y = xp[:, 0:t] * w[0]
    for j in range(1, CONV_W):
        y = y + xp[:, j:j + t] * w[j]
    return y, xp[:, t:]


def to_chunks4(x, c):
    b, t, h, d = x.shape
    return x.reshape(b, t // c, c, h, d).transpose(1, 0, 3, 2, 4)


def to_chunks3(x, c):
    b, t, h = x.shape
    return x.reshape(b, t // c, c, h).transpose(1, 0, 3, 2)


def from_chunks4(o):
    n, b, h, c, d = o.shape
    return o.transpose(1, 0, 3, 2, 4).reshape(b, n * c, h, d)


def gated_delta_chunked(q, k, v, beta, g, s0):
    t = q.shape[1]
    c = math.gcd(t, CHUNK)
    qc, kc, vc = to_chunks4(q, c), to_chunks4(k, c), to_chunks4(v, c)
    bc, gc = to_chunks3(beta, c), to_chunks3(g, c)
    gcum = jnp.cumsum(gc, axis=-1)
    idx = jnp.arange(c)
    incl = idx[:, None] >= idx[None, :]
    strict = idx[:, None] > idx[None, :]
    diff = gcum[..., :, None] - gcum[..., None, :]
    decay = jnp.where(incl, jnp.exp(jnp.where(incl, diff, 0.0)), 0.0)
    kb = kc * bc[..., None]
    a_mat = jnp.where(strict, jnp.einsum('nbhid,nbhjd->nbhij', kb, kc) * decay, 0.0) + jnp.eye(c, dtype=q.dtype)
    u = lax.linalg.triangular_solve(a_mat, vc * bc[..., None], left_side=True, lower=True, unit_diagonal=True)
    w = lax.linalg.triangular_solve(a_mat, kb * jnp.exp(gcum)[..., None], left_side=True, lower=True, unit_diagonal=True)
    qk = jnp.einsum('nbhid,nbhjd->nbhij', qc, kc) * decay
    q_dec = qc * jnp.exp(gcum)[..., None]
    k_dec = kc * jnp.exp(gcum[..., -1:] - gcum)[..., None]
    g_last = jnp.exp(gcum[..., -1])

    def step(s, xs):
        u_n, w_n, qk_n, qd_n, kd_n, gl_n = xs
        v_new = u_n - jnp.einsum('bhcd,bhde->bhce', w_n, s)
        o = jnp.einsum('bhcd,bhde->bhce', qd_n, s) + jnp.einsum('bhij,bhje->bhie', qk_n, v_new)
        s = s * gl_n[..., None, None] + jnp.einsum('bhcd,bhce->bhde', kd_n, v_new)
        return s, o

    s, o = lax.scan(step, s0, (u, w, qk, q_dec, k_dec, g_last))
    return from_chunks4(o), s


def mlstm_chunked(q, k, v, ig, lf, c0, n0, m0):
    t = q.shape[1]
    c = math.gcd(t, CHUNK)
    qc, kc, vc = to_chunks4(q, c), to_chunks4(k, c), to_chunks4(v, c)
    igc, lfc = to_chunks3(ig, c), to_chunks3(lf, c)
    bcum = jnp.cumsum(lfc, axis=-1)
    idx = jnp.arange(c)
    incl = idx[:, None] >= idx[None, :]
    dlog = jnp.where(incl, bcum[..., :, None] - bcum[..., None, :] + igc[..., None, :], -jnp.inf)
    dmax = jnp.max(dlog, axis=-1)
    a_end = bcum[..., -1:] - bcum + igc
    a_max = jnp.max(a_end, axis=-1)
    qk = jnp.einsum('nbhid,nbhjd->nbhij', qc, kc)

    def step(carry, xs):
        cs, ns, ms = carry
        qn, kn, vn, bn, dl, dm, qk_n, an, am = xs
        inter = bn + ms[..., None]
        mt = jnp.maximum(inter, dm)
        sc = qk_n * jnp.exp(dl - mt[..., None])
        w_inter = jnp.exp(inter - mt)
        num = w_inter[..., None] * jnp.einsum('bhcd,bhde->bhce', qn, cs) + jnp.einsum('bhij,bhje->bhie', sc, vn)
        den = w_inter * jnp.einsum('bhcd,bhd->bhc', qn, ns) + jnp.sum(sc, axis=-1)
        h = num / jnp.maximum(jnp.abs(den), jnp.exp(-mt))[..., None]
        m_new = jnp.maximum(bn[..., -1] + ms, am)
        keep = jnp.exp(bn[..., -1] + ms - m_new)
        kw = kn * jnp.exp(an - m_new[..., None])[..., None]
        cs = keep[..., None, None] * cs + jnp.einsum('bhcd,bhce->bhde', kw, vn)
        ns = keep[..., None] * ns + jnp.sum(kw, axis=-2)
        return (cs, ns, m_new), h

    (cs, ns, ms), h = lax.scan(step, (c0, n0, m0), (qc, kc, vc, bcum, dlog, dmax, qk, a_end, a_max))
    return from_chunks4(h), cs, ns, ms


def t5_bucket(dist):
    n = jnp.maximum(dist, 0)
    max_exact = NUM_BUCKETS // 2
    nf = jnp.maximum(n, 1).astype(jnp.float32)
    large = max_exact + (jnp.log(nf / max_exact) / math.log(MAX_DISTANCE / max_exact)
                         * (NUM_BUCKETS - max_exact)).astype(jnp.int32)
    large = jnp.minimum(large, NUM_BUCKETS - 1)
    return jnp.where(n < max_exact, n, large)


def local_gather(k, v):
    take = jax.vmap(lambda a, s: a[s])

    def gather(sel):
        return take(k, sel), take(v, sel)
    return gather


def paged_gather(cache_k, cache_v, l, page_table, k_new, v_new):
    b, t = k_new.shape[0], k_new.shape[1]
    past_len = page_table.shape[1] * PAGE_SIZE
    take = jax.vmap(lambda a, s: a[s])

    def gather(sel):
        is_past = (sel < past_len)[..., None, None]
        ps = jnp.minimum(sel, past_len - 1)
        phys = page_table[jnp.arange(b)[:, None, None], ps // PAGE_SIZE]
        off = ps % PAGE_SIZE
        ns = jnp.clip(sel - past_len, 0, t - 1)
        k_s = jnp.where(is_past, cache_k[l, phys, off].astype(k_new.dtype), take(k_new, ns))
        v_s = jnp.where(is_past, cache_v[l, phys, off].astype(v_new.dtype), take(v_new, ns))
        return k_s, v_s
    return gather


def dsa_attend(q, q_idx, w_idx, k_idx, gather_kv, q_pos0, rel_bias):
    b, t, h, d = q.shape
    n_keys = k_idx.shape[1]
    topk = min(TOPK_MAX, n_keys // 4)
    qb = math.gcd(t, max(1, Q_BLOCK // b))
    kpos = jnp.arange(n_keys)
    kidx_f = k_idx.astype(jnp.float32)

    def block(i):
        start = i * qb
        q_b = lax.dynamic_slice_in_dim(q, start, qb, axis=1)
        qi_b = lax.dynamic_slice_in_dim(q_idx, start, qb, axis=1).astype(jnp.float32)
        wi_b = lax.dynamic_slice_in_dim(w_idx, start, qb, axis=1).astype(jnp.float32)
        qpos = q_pos0 + start + jnp.arange(qb)
        score = jnp.einsum('bqhd,bsd->bqsh', qi_b, kidx_f) * IDX_DIM ** -0.5
        score = jnp.einsum('bqsh,bqh->bqs', jax.nn.relu(score), wi_b * IDX_HEADS ** -0.5)
        score = jnp.where((kpos[None, :] <= qpos[:, None])[None], score, -jnp.inf)
        _, sel = lax.top_k(score, topk)
        valid = sel <= qpos[None, :, None]
        k_s, v_s = gather_kv(sel)
        bias = rel_bias[t5_bucket(qpos[None, :, None] - sel)].astype(jnp.float32)
        logits = jnp.einsum('bqhd,bqkhd->bqhk', q_b, k_s).astype(jnp.float32) * d ** -0.5
        logits = logits + bias.transpose(0, 1, 3, 2)
        logits = jnp.where(valid[:, :, None, :], logits, -jnp.inf)
        p = jax.nn.softmax(logits, axis=-1)
        return jnp.einsum('bqhk,bqkhd->bqhd', p.astype(v_s.dtype), v_s)

    out = lax.map(block, jnp.arange(t // qb))
    return out.transpose(1, 0, 2, 3, 4).reshape(b, t, h, d)


def gdn_branch(qkv_raw, z, b_raw, a_raw, conv_w, a_log, dt_bias, norm_g, s0, conv0):
    b, t, _ = qkv_raw.shape
    conv_out, conv_new = causal_conv(qkv_raw, conv0, conv_w)
    act = jax.nn.silu(conv_out).astype(jnp.float32).reshape(b, t, 3, N_HEADS, HEAD_DIM)
    q = l2norm(act[:, :, 0]) * HEAD_DIM ** -0.5
    k = l2norm(act[:, :, 1])
    v = act[:, :, 2]
    beta = jax.nn.sigmoid(b_raw.astype(jnp.float32))
    g = -jnp.exp(a_log.astype(jnp.float32)) * jax.nn.softplus(a_raw.astype(jnp.float32) + dt_bias.astype(jnp.float32))
    o, s_new = gated_delta_chunked(q, k, v, beta, g, s0.astype(jnp.float32))
    o = rmsnorm(o, norm_g) * jax.nn.silu(z.astype(jnp.float32).reshape(b, t, N_HEADS, HEAD_DIM))
    return o.reshape(b, t, MIX_WIDTH).astype(qkv_raw.dtype), s_new, conv_new


def mlstm_branch(q_raw, k_raw, v_raw, o_raw, i_raw, f_raw, norm_g, c0, n0, m0):
    b, t, _ = q_raw.shape
    shp = (b, t, N_HEADS, HEAD_DIM)
    q = q_raw.astype(jnp.float32).reshape(shp) * HEAD_DIM ** -0.5
    k = k_raw.astype(jnp.float32).reshape(shp)
    v = v_raw.astype(jnp.float32).reshape(shp)
    ig = i_raw.astype(jnp.float32)
    lf = jax.nn.log_sigmoid(f_raw.astype(jnp.float32))
    h, c_new, n_new, m_new = mlstm_chunked(q, k, v, ig, lf, c0.astype(jnp.float32),
                                           n0.astype(jnp.float32), m0.astype(jnp.float32))
    h = rmsnorm(h, norm_g) * jax.nn.sigmoid(o_raw.astype(jnp.float32).reshape(shp))
    return h.reshape(b, t, MIX_WIDTH).astype(q_raw.dtype), c_new, n_new, m_new


def trunk_layer(x, l, W, attend, init):
    gdn_s0, conv0, c0, n0, m0 = init
    b, t, _ = x.shape
    x = x + 0.5 * swiglu(rmsnorm(x, W['norm_g'][l, 0]), W['ffa_w13'][l], W['ffa_w2'][l])
    h = rmsnorm(x, W['norm_g'][l, 1])
    proj = h @ W['w_in'][l] + W['b_in'][l]
    (g_qkv, g_z, g_b, g_a, m_q, m_k, m_v, m_o, m_i, m_f,
     a_q, a_k, a_v, i_q, i_w, i_k, gates) = _split_in(proj)
    o_a, s_new, conv_new = gdn_branch(g_qkv, g_z, g_b, g_a, W['gdn_conv_w'][l], W['gdn_a_log'][l],
                                      W['gdn_dt_bias'][l], W['gdn_norm_g'][l], gdn_s0, conv0)
    o_b, c_new, n_new, m_new = mlstm_branch(m_q, m_k, m_v, m_o, m_i, m_f, W['mlstm_norm_g'][l], c0, n0, m0)
    q_att = a_q.reshape(b, t, N_HEADS, HEAD_DIM)
    k_att = a_k.reshape(b, t, N_HEADS, HEAD_DIM)
    v_att = a_v.reshape(b, t, N_HEADS, HEAD_DIM)
    k_idx = layernorm(i_k, W['idx_ln_g'][l], W['idx_ln_b'][l])
    q_idx = i_q.reshape(b, t, IDX_HEADS, IDX_DIM)
    o_c = attend(l, q_att, k_att, v_att, q_idx, i_w, k_idx).reshape(b, t, MIX_WIDTH)
    branches = jnp.stack([o_a, o_b, o_c], axis=2)
    y_br = jnp.einsum('btnw,nwd->btnd', branches, W['w_branch'][l])
    gate = jax.nn.sigmoid(gates.reshape(b, t, N_BRANCH, D_MODEL))
    merged = jnp.sum(gate * y_br, axis=2)
    x = x + merged @ W['w_out'][l]
    x = x + 0.5 * swiglu(rmsnorm(x, W['norm_g'][l, 2]), W['ffb_w13'][l], W['ffb_w2'][l])
    return x, (k_att, v_att, k_idx, s_new, conv_new, c_new, n_new, m_new)


def setup_inputs(seed: int = 0) -> dict:
    key = jax.random.key(seed)
    ks = jax.random.split(key, 32)
    f32 = jnp.float32
    n_pages = PAST_LEN // PAGE_SIZE
    used = DEC_BATCH * n_pages
    n_pool = used + max(1, used // 4)

    def nrm(k, shape, s):
        return jax.random.normal(k, shape, f32) * s

    x_prompt = nrm(ks[0], (BATCH, SEQ, D_MODEL), 1.0)
    x_sample = nrm(ks[1], (DEC_BATCH, DEC_SEQ, D_MODEL), 1.0)
    cache_k = nrm(ks[2], (DEPTH, n_pool, PAGE_SIZE, N_HEADS, HEAD_DIM), 1.0)
    cache_v = nrm(ks[3], (DEPTH, n_pool, PAGE_SIZE, N_HEADS, HEAD_DIM), 1.0)
    cache_idx_k = nrm(ks[4], (DEPTH, n_pool, PAGE_SIZE, IDX_DIM), 1.0)
    state_gdn = nrm(ks[5], (DEPTH, DEC_BATCH, N_HEADS, HEAD_DIM, HEAD_DIM), HEAD_DIM ** -0.5)
    state_gdn_conv = nrm(ks[6], (DEPTH, DEC_BATCH, CONV_W - 1, 3 * MIX_WIDTH), 1.0)
    state_mlstm_c = nrm(ks[7], (DEPTH, DEC_BATCH, N_HEADS, HEAD_DIM, HEAD_DIM), 0.5)
    state_mlstm_n = nrm(ks[8], (DEPTH, DEC_BATCH, N_HEADS, HEAD_DIM), 0.5)
    state_mlstm_m = nrm(ks[9], (DEPTH, DEC_BATCH, N_HEADS), 1.0)
    page_table = jax.random.permutation(ks[10], n_pool)[:used].reshape(DEC_BATCH, n_pages).astype(jnp.int32)
    norm_g = 1.0 + nrm(ks[11], (DEPTH, 3, D_MODEL), 0.02)
    final_g = 1.0 + nrm(ks[12], (D_MODEL,), 0.02)
    ffa_w13 = nrm(ks[13], (DEPTH, D_MODEL, 2 * D_FF), D_MODEL ** -0.5)
    ffa_w2 = nrm(ks[14], (DEPTH, D_FF, D_MODEL), D_FF ** -0.5)
    ffb_w13 = nrm(ks[15], (DEPTH, D_MODEL, 2 * D_FF), D_MODEL ** -0.5)
    ffb_w2 = nrm(ks[16], (DEPTH, D_FF, D_MODEL), D_FF ** -0.5)
    w_in = nrm(ks[17], (DEPTH, D_MODEL, N_IN), D_MODEL ** -0.5)
    f_off = sum(SEG_WIDTHS[:9])
    b_in = nrm(ks[18], (DEPTH, N_IN), 0.02).at[:, f_off:f_off + N_HEADS].add(FORGET_BIAS)
    gdn_conv_w = nrm(ks[19], (DEPTH, CONV_W, 3 * MIX_WIDTH), CONV_W ** -0.5)
    gdn_a_log = jnp.log(jax.random.uniform(ks[20], (DEPTH, N_HEADS), f32, 1.0, 16.0))
    dt = jnp.exp(jax.random.uniform(ks[21], (DEPTH, N_HEADS), f32, math.log(1e-3), math.log(1e-1)))
    gdn_dt_bias = dt + jnp.log(-jnp.expm1(-dt))
    gdn_norm_g = 1.0 + nrm(ks[22], (DEPTH, HEAD_DIM), 0.02)
    mlstm_norm_g = 1.0 + nrm(ks[23], (DEPTH, HEAD_DIM), 0.02)
    idx_ln_g = 1.0 + nrm(ks[24], (DEPTH, IDX_DIM), 0.02)
    idx_ln_b = nrm(ks[25], (DEPTH, IDX_DIM), 0.02)
    rel_bias = nrm(ks[26], (NUM_BUCKETS, N_HEADS), 0.5)
    w_branch = nrm(ks[27], (DEPTH, N_BRANCH, MIX_WIDTH, D_MODEL), MIX_WIDTH ** -0.5)
    w_out = nrm(ks[28], (DEPTH, D_MODEL, D_MODEL), D_MODEL ** -0.5)
    return {
        'x_prompt': x_prompt, 'x_sample': x_sample,
        'cache_k': cache_k, 'cache_v': cache_v, 'cache_idx_k': cache_idx_k,
        'state_gdn': state_gdn, 'state_gdn_conv': state_gdn_conv,
        'state_mlstm_c': state_mlstm_c, 'state_mlstm_n': state_mlstm_n, 'state_mlstm_m': state_mlstm_m,
        'page_table': page_table,
        'norm_g': norm_g, 'final_g': final_g,
        'ffa_w13': ffa_w13, 'ffa_w2': ffa_w2, 'ffb_w13': ffb_w13, 'ffb_w2': ffb_w2,
        'w_in': w_in, 'b_in': b_in,
        'gdn_conv_w': gdn_conv_w, 'gdn_a_log': gdn_a_log, 'gdn_dt_bias': gdn_dt_bias,
        'gdn_norm_g': gdn_norm_g, 'mlstm_norm_g': mlstm_norm_g,
        'idx_ln_g': idx_ln_g, 'idx_ln_b': idx_ln_b, 'rel_bias': rel_bias,
        'w_branch': w_branch, 'w_out': w_out,
    }


def reference(x_prompt, x_sample, cache_k, cache_v, cache_idx_k, state_gdn, state_gdn_conv,
              state_mlstm_c, state_mlstm_n, state_mlstm_m, page_table, norm_g, final_g,
              ffa_w13, ffa_w2, ffb_w13, ffb_w2, w_in, b_in, gdn_conv_w, gdn_a_log, gdn_dt_bias,
              gdn_norm_g, mlstm_norm_g, idx_ln_g, idx_ln_b, rel_bias, w_branch, w_out):
    W = dict(norm_g=norm_g, ffa_w13=ffa_w13, ffa_w2=ffa_w2, ffb_w13=ffb_w13, ffb_w2=ffb_w2,
             w_in=w_in, b_in=b_in, gdn_conv_w=gdn_conv_w, gdn_a_log=gdn_a_log, gdn_dt_bias=gdn_dt_bias,
             gdn_norm_g=gdn_norm_g, mlstm_norm_g=mlstm_norm_g, idx_ln_g=idx_ln_g, idx_ln_b=idx_ln_b,
             w_branch=w_branch, w_out=w_out)
    past_len = page_table.shape[1] * PAGE_SIZE

    def prompt_attend(l, q, k, v, q_idx, w_idx, k_idx):
        return dsa_attend(q, q_idx, w_idx, k_idx, local_gather(k, v), 0, rel_bias)

    def sample_attend(l, q, k, v, q_idx, w_idx, k_idx):
        b = q.shape[0]
        past_idx = cache_idx_k[l, page_table].reshape(b, past_len, IDX_DIM).astype(k_idx.dtype)
        k_idx_all = jnp.concatenate([past_idx, k_idx], axis=1)
        gather = paged_gather(cache_k, cache_v, l, page_table, k, v)
        return dsa_attend(q, q_idx, w_idx, k_idx_all, gather, past_len, rel_bias)

    bp = x_prompt.shape[0]
    x = x_prompt
    p_st = []
    for l in range(DEPTH):
        init = (jnp.zeros((bp, N_HEADS, HEAD_DIM, HEAD_DIM), jnp.float32),
                jnp.zeros((bp, CONV_W - 1, 3 * MIX_WIDTH), x.dtype),
                jnp.zeros((bp, N_HEADS, HEAD_DIM, HEAD_DIM), jnp.float32),
                jnp.zeros((bp, N_HEADS, HEAD_DIM), jnp.float32),
                jnp.zeros((bp, N_HEADS), jnp.float32))
        x, st = trunk_layer(x, l, W, prompt_attend, init)
        p_st.append(st)
    y_prompt = rmsnorm(x, final_g)

    x = x_sample
    s_st = []
    for l in range(DEPTH):
        init = (state_gdn[l], state_gdn_conv[l], state_mlstm_c[l], state_mlstm_n[l], state_mlstm_m[l])
        x, st = trunk_layer(x, l, W, sample_attend, init)
        s_st.append(st)
    y_sample = rmsnorm(x, final_g)

    def stacked(states, i):
        return jnp.stack([s[i] for s in states], axis=0)

    p_k, p_v, p_idx_k = stacked(p_st, 0), stacked(p_st, 1), stacked(p_st, 2)
    p_gdn, p_gdn_conv = stacked(p_st, 3), stacked(p_st, 4)
    p_mlstm_c, p_mlstm_n, p_mlstm_m = stacked(p_st, 5), stacked(p_st, 6), stacked(p_st, 7)
    s_k, s_v, s_idx_k = stacked(s_st, 0), stacked(s_st, 1), stacked(s_st, 2)
    s_gdn, s_gdn_conv = stacked(s_st, 3), stacked(s_st, 4)
    s_mlstm_c, s_mlstm_n, s_mlstm_m = stacked(s_st, 5), stacked(s_st, 6), stacked(s_st, 7)
    return (y_prompt, y_sample,
            p_k, p_v, p_idx_k, p_gdn, p_gdn_conv, p_mlstm_c, p_mlstm_n, p_mlstm_m,
            s_k, s_v, s_idx_k, s_gdn, s_gdn_conv, s_mlstm_c, s_mlstm_n, s_mlstm_m)
```

```python
import functools
import math

import jax
import jax.numpy as jnp
from jax import lax
from jax.experimental import pallas as pl
from jax.experimental.pallas import tpu as pltpu

F32 = jnp.float32
BF16 = jnp.bfloat16

HEAD_DIM = 128
CONV_W = 4
CHUNK = 64
IDX_HEADS = 16
IDX_DIM = 64
TOPK_MAX = 256
Q_BLOCK = 128
NUM_BUCKETS = 32
MAX_DISTANCE = 128
PAGE_SIZE = 128
N_BRANCH = 3
EPS = 1e-6

LANE = 128
VMEM_LIMIT = 56 * 1024 * 1024


def _round_up(a, b):
    return (a + b - 1) // b * b


def _rms_rows(x, g):
    return x * lax.rsqrt(jnp.mean(x * x, axis=-1, keepdims=True) + EPS) * g


def _ffn_kernel(x_ref, g_ref, w1_ref, w3_ref, w2_ref, o_ref, n_ref):
    j = pl.program_id(1)

    @pl.when(j == 0)
    def _():
        n_ref[...] = _rms_rows(x_ref[...], g_ref[...]).astype(BF16)
        o_ref[...] = jnp.zeros_like(o_ref)

    n = n_ref[...]
    g = jnp.dot(n, w1_ref[...], preferred_element_type=F32)
    u = jnp.dot(n, w3_ref[...], preferred_element_type=F32)
    a = (g * jax.nn.sigmoid(g) * u).astype(BF16)
    o_ref[...] += jnp.dot(a, w2_ref[...], preferred_element_type=F32)

    @pl.when(j == pl.num_programs(1) - 1)
    def _():
        o_ref[...] = x_ref[...] + 0.5 * o_ref[...]


def _ffn(x, g, w13p, w2p, *, tm=512, tf=512):
    m, d = x.shape
    fp = w2p.shape[0]
    nj = fp // tf
    return pl.pallas_call(
        _ffn_kernel,
        grid=(m // tm, nj),
        in_specs=[
            pl.BlockSpec((tm, d), lambda i, j: (i, 0)),
            pl.BlockSpec((1, d), lambda i, j: (0, 0)),
            pl.BlockSpec((d, tf), lambda i, j: (0, j)),
            pl.BlockSpec((d, tf), lambda i, j: (0, j + nj)),
            pl.BlockSpec((tf, d), lambda i, j: (j, 0)),
        ],
        out_specs=pl.BlockSpec((tm, d), lambda i, j: (i, 0)),
        out_shape=jax.ShapeDtypeStruct((m, d), F32),
        scratch_shapes=[pltpu.VMEM((tm, d), BF16)],
        compiler_params=pltpu.CompilerParams(
            dimension_semantics=("parallel", "arbitrary"), vmem_limit_bytes=VMEM_LIMIT),
        name="ffn",
    )(x, g, w13p, w13p, w2p)


def _inproj_kernel(x_ref, g_ref, w_ref, b_ref, o_ref, n_ref):
    @pl.when(pl.program_id(1) == 0)
    def _():
        n_ref[...] = _rms_rows(x_ref[...], g_ref[...]).astype(BF16)

    o_ref[...] = jnp.dot(n_ref[...], w_ref[...], preferred_element_type=F32) + b_ref[...]


def _inproj(x, g, w, b, *, tm=512, tn=1024):
    m, d = x.shape
    n = w.shape[1]
    return pl.pallas_call(
        _inproj_kernel,
        grid=(m // tm, n // tn),
        in_specs=[
            pl.BlockSpec((tm, d), lambda i, j: (i, 0)),
            pl.BlockSpec((1, d), lambda i, j: (0, 0)),
            pl.BlockSpec((d, tn), lambda i, j: (0, j)),
            pl.BlockSpec((1, tn), lambda i, j: (0, j)),
        ],
        out_specs=pl.BlockSpec((tm, tn), lambda i, j: (i, j)),
        out_shape=jax.ShapeDtypeStruct((m, n), F32),
        scratch_shapes=[pltpu.VMEM((tm, d), BF16)],
        compiler_params=pltpu.CompilerParams(
            dimension_semantics=("parallel", "arbitrary"), vmem_limit_bytes=VMEM_LIMIT),
        name="inproj",
    )(x, g, w, b)


def _inproj_small_kernel(x_ref, g_ref, w_ref, b_ref, o_ref):
    n = _rms_rows(x_ref[...], g_ref[...])
    o_ref[...] = jnp.dot(n, w_ref[...], preferred_element_type=F32,
                         precision=lax.Precision.HIGHEST) + b_ref[...]


def _inproj_small(x, g, w, b, *, tm=512):
    m, d = x.shape
    n = w.shape[1]
    return pl.pallas_call(
        _inproj_small_kernel,
        grid=(m // tm,),
        in_specs=[
            pl.BlockSpec((tm, d), lambda i: (i, 0)),
            pl.BlockSpec((1, d), lambda i: (0, 0)),
            pl.BlockSpec((d, n), lambda i: (0, 0)),
            pl.BlockSpec((1, n), lambda i: (0, 0)),
        ],
        out_specs=pl.BlockSpec((tm, n), lambda i: (i, 0)),
        out_shape=jax.ShapeDtypeStruct((m, n), F32),
        compiler_params=pltpu.CompilerParams(
            dimension_semantics=("parallel",), vmem_limit_bytes=VMEM_LIMIT),
        name="inproj_small",
    )(x, g, w, b)


def _merge_kernel(x_ref, g_ref, oa_ref, ob_ref, oc_ref, wg0_ref, wg1_ref, wg2_ref,
                  bg0_ref, bg1_ref, bg2_ref, wb0_ref, wb1_ref, wb2_ref, wo_ref, o_ref, n_ref):
    j = pl.program_id(1)

    @pl.when(j == 0)
    def _():
        n_ref[...] = _rms_rows(x_ref[...], g_ref[...]).astype(BF16)
        o_ref[...] = jnp.zeros_like(o_ref)

    n = n_ref[...]
    merged = None
    for o_r, wg_r, bg_r, wb_r in ((oa_ref, wg0_ref, bg0_ref, wb0_ref),
                                  (ob_ref, wg1_ref, bg1_ref, wb1_ref),
                                  (oc_ref, wg2_ref, bg2_ref, wb2_ref)):
        gate = jax.nn.sigmoid(jnp.dot(n, wg_r[...], preferred_element_type=F32) + bg_r[...])
        y = jnp.dot(o_r[...], wb_r[0], preferred_element_type=F32)
        merged = gate * y if merged is None else merged + gate * y
    o_ref[...] += jnp.dot(merged.astype(BF16), wo_ref[...], preferred_element_type=F32)

    @pl.when(j == pl.num_programs(1) - 1)
    def _():
        o_ref[...] = x_ref[...] + o_ref[...]


def _merge(x, g, oa, ob, oc, wg, bg, wb, wo, *, tm=512, tn=512):
    m, d = x.shape
    w = oa.shape[1]
    nj = d // tn
    wg_specs = [pl.BlockSpec((d, tn), functools.partial(lambda i, j, n: (0, n * nj + j), n=n))
                for n in range(N_BRANCH)]
    bg_specs = [pl.BlockSpec((1, tn), functools.partial(lambda i, j, n: (0, n * nj + j), n=n))
                for n in range(N_BRANCH)]
    wb_specs = [pl.BlockSpec((1, w, tn), functools.partial(lambda i, j, n: (n, 0, j), n=n))
                for n in range(N_BRANCH)]
    o_spec = pl.BlockSpec((tm, w), lambda i, j: (i, 0))
    return pl.pallas_call(
        _merge_kernel,
        grid=(m // tm, nj),
        in_specs=[pl.BlockSpec((tm, d), lambda i, j: (i, 0)),
                  pl.BlockSpec((1, d), lambda i, j: (0, 0)),
                  o_spec, o_spec, o_spec,
                  *wg_specs, *bg_specs, *wb_specs,
                  pl.BlockSpec((tn, d), lambda i, j: (j, 0))],
        out_specs=pl.BlockSpec((tm, d), lambda i, j: (i, 0)),
        out_shape=jax.ShapeDtypeStruct((m, d), F32),
        scratch_shapes=[pltpu.VMEM((tm, d), BF16)],
        compiler_params=pltpu.CompilerParams(
            dimension_semantics=("parallel", "arbitrary"), vmem_limit_bytes=VMEM_LIMIT),
        name="merge",
    )(x, g, oa, ob, oc, wg, wg, wg, bg, bg, bg, wb, wb, wb, wo)


def _norm_kernel(x_ref, g_ref, o_ref):
    o_ref[...] = _rms_rows(x_ref[...], g_ref[...])


def _final_norm(x, g, *, tm=512):
    m, d = x.shape
    return pl.pallas_call(
        _norm_kernel,
        grid=(m // tm,),
        in_specs=[pl.BlockSpec((tm, d), lambda i: (i, 0)), pl.BlockSpec((1, d), lambda i: (0, 0))],
        out_specs=pl.BlockSpec((tm, d), lambda i: (i, 0)),
        out_shape=jax.ShapeDtypeStruct((m, d), F32),
        compiler_params=pltpu.CompilerParams(dimension_semantics=("parallel",)),
        name="final_norm",
    )(x, g)


def _l2norm(x):
    return x * lax.rsqrt(jnp.sum(x * x, axis=-1, keepdims=True) + EPS)


def _rmsnorm(x, g):
    return x * lax.rsqrt(jnp.mean(x * x, axis=-1, keepdims=True) + EPS) * g


def _layernorm(x, g, b):
    mu = jnp.mean(x, axis=-1, keepdims=True)
    var = jnp.mean(jnp.square(x - mu), axis=-1, keepdims=True)
    return (x - mu) * lax.rsqrt(var + EPS) * g + b


def _causal_conv(u, state, w):
    t = u.shape[1]
    xp = jnp.concatenate([state, u], axis=1)
    y = xp[:, 0:t] * w[0]
    for j in range(1, CONV_W):
        y = y + xp[:, j:j + t] * w[j]
    return y, xp[:, t:]


def _to_chunks4(x, c):
    b, t, h, d = x.shape
    return x.reshape(b, t // c, c, h, d).transpose(1, 0, 3, 2, 4)


def _to_chunks3(x, c):
    b, t, h = x.shape
    return x.reshape(b, t // c, c, h).transpose(1, 0, 3, 2)


def _from_chunks4(o):
    n, b, h, c, d = o.shape
    return o.transpose(1, 0, 3, 2, 4).reshape(b, n * c, h, d)


def _gated_delta_chunked(q, k, v, beta, g, s0):
    t = q.shape[1]
    c = math.gcd(t, CHUNK)
    qc, kc, vc = _to_chunks4(q, c), _to_chunks4(k, c), _to_chunks4(v, c)
    bc, gc = _to_chunks3(beta, c), _to_chunks3(g, c)
    gcum = jnp.cumsum(gc, axis=-1)
    idx = jnp.arange(c)
    incl = idx[:, None] >= idx[None, :]
    strict = idx[:, None] > idx[None, :]
    diff = gcum[..., :, None] - gcum[..., None, :]
    decay = jnp.where(incl, jnp.exp(jnp.where(incl, diff, 0.0)), 0.0)
    kb = kc * bc[..., None]
    a_mat = jnp.where(strict, jnp.einsum('nbhid,nbhjd->nbhij', kb, kc) * decay, 0.0) + jnp.eye(c, dtype=q.dtype)
    u = lax.linalg.triangular_solve(a_mat, vc * bc[..., None], left_side=True, lower=True, unit_diagonal=True)
    w = lax.linalg.triangular_solve(a_mat, kb * jnp.exp(gcum)[..., None], left_side=True, lower=True,
                                    unit_diagonal=True)
    qk = jnp.einsum('nbhid,nbhjd->nbhij', qc, kc) * decay
    q_dec = qc * jnp.exp(gcum)[..., None]
    k_dec = kc * jnp.exp(gcum[..., -1:] - gcum)[..., None]
    g_last = jnp.exp(gcum[..., -1])

    def step(s, xs):
        u_n, w_n, qk_n, qd_n, kd_n, gl_n = xs
        v_new = u_n - jnp.einsum('bhcd,bhde->bhce', w_n, s)
        o = jnp.einsum('bhcd,bhde->bhce', qd_n, s) + jnp.einsum('bhij,bhje->bhie', qk_n, v_new)
        s = s * gl_n[..., None, None] + jnp.einsum('bhcd,bhce->bhde', kd_n, v_new)
        return s, o

    s, o = lax.scan(step, s0, (u, w, qk, q_dec, k_dec, g_last))
    return _from_chunks4(o), s


def _mlstm_chunked(q, k, v, ig, lf, c0, n0, m0):
    t = q.shape[1]
    c = math.gcd(t, CHUNK)
    qc, kc, vc = _to_chunks4(q, c), _to_chunks4(k, c), _to_chunks4(v, c)
    igc, lfc = _to_chunks3(ig, c), _to_chunks3(lf, c)
    bcum = jnp.cumsum(lfc, axis=-1)
    idx = jnp.arange(c)
    incl = idx[:, None] >= idx[None, :]
    dlog = jnp.where(incl, bcum[..., :, None] - bcum[..., None, :] + igc[..., None, :], -jnp.inf)
    dmax = jnp.max(dlog, axis=-1)
    a_end = bcum[..., -1:] - bcum + igc
    a_max = jnp.max(a_end, axis=-1)
    qk = jnp.einsum('nbhid,nbhjd->nbhij', qc, kc)

    def step(carry, xs):
        cs, ns, ms = carry
        qn, kn, vn, bn, dl, dm, qk_n, an, am = xs
        inter = bn + ms[..., None]
        mt = jnp.maximum(inter, dm)
        sc = qk_n * jnp.exp(dl - mt[..., None])
        w_inter = jnp.exp(inter - mt)
        num = w_inter[..., None] * jnp.einsum('bhcd,bhde->bhce', qn, cs) + jnp.einsum('bhij,bhje->bhie', sc, vn)
        den = w_inter * jnp.einsum('bhcd,bhd->bhc', qn, ns) + jnp.sum(sc, axis=-1)
        h = num / jnp.maximum(jnp.abs(den), jnp.exp(-mt))[..., None]
        m_new = jnp.maximum(bn[..., -1] + ms, am)
        keep = jnp.exp(bn[..., -1] + ms - m_new)
        kw = kn * jnp.exp(an - m_new[..., None])[..., None]
        cs = keep[..., None, None] * cs + jnp.einsum('bhcd,bhce->bhde', kw, vn)
        ns = keep[..., None] * ns + jnp.sum(kw, axis=-2)
        return (cs, ns, m_new), h

    (cs, ns, ms), h = lax.scan(step, (c0, n0, m0), (qc, kc, vc, bcum, dlog, dmax, qk, a_end, a_max))
    return _from_chunks4(h), cs, ns, ms


def _t5_bucket(dist):
    n = jnp.maximum(dist, 0)
    max_exact = NUM_BUCKETS // 2
    nf = jnp.maximum(n, 1).astype(F32)
    large = max_exact + (jnp.log(nf / max_exact) / math.log(MAX_DISTANCE / max_exact)
                         * (NUM_BUCKETS - max_exact)).astype(jnp.int32)
    large = jnp.minimum(large, NUM_BUCKETS - 1)
    return jnp.where(n < max_exact, n, large)


def _local_gather(k, v):
    take = jax.vmap(lambda a, s: a[s])

    def gather(sel):
        return take(k, sel), take(v, sel)
    return gather


def _paged_gather(cache_k, cache_v, l, page_table, k_new, v_new):
    b, t = k_new.shape[0], k_new.shape[1]
    past_len = page_table.shape[1] * PAGE_SIZE
    take = jax.vmap(lambda a, s: a[s])

    def gather(sel):
        is_past = (sel < past_len)[..., None, None]
        ps = jnp.minimum(sel, past_len - 1)
        phys = page_table[jnp.arange(b)[:, None, None], ps // PAGE_SIZE]
        off = ps % PAGE_SIZE
        ns = jnp.clip(sel - past_len, 0, t - 1)
        k_s = jnp.where(is_past, cache_k[l, phys, off], take(k_new, ns))
        v_s = jnp.where(is_past, cache_v[l, phys, off], take(v_new, ns))
        return k_s, v_s
    return gather


def _dsa_attend(q, q_idx, w_idx, k_idx, gather_kv, q_pos0, rel_bias):
    b, t, h, d = q.shape
    n_keys = k_idx.shape[1]
    topk = min(TOPK_MAX, n_keys // 4)
    qb = math.gcd(t, max(1, Q_BLOCK // b))
    kpos = jnp.arange(n_keys)

    def block(i):
        start = i * qb
        q_b = lax.dynamic_slice_in_dim(q, start, qb, axis=1)
        qi_b = lax.dynamic_slice_in_dim(q_idx, start, qb, axis=1)
        wi_b = lax.dynamic_slice_in_dim(w_idx, start, qb, axis=1)
        qpos = q_pos0 + start + jnp.arange(qb)
        score = jnp.einsum('bqhd,bsd->bqsh', qi_b, k_idx) * IDX_DIM ** -0.5
        score = jnp.einsum('bqsh,bqh->bqs', jax.nn.relu(score), wi_b * IDX_HEADS ** -0.5)
        score = jnp.where((kpos[None, :] <= qpos[:, None])[None], score, -jnp.inf)
        _, sel = lax.top_k(score, topk)
        valid = sel <= qpos[None, :, None]
        k_s, v_s = gather_kv(sel)
        bias = rel_bias[_t5_bucket(qpos[None, :, None] - sel)]
        logits = jnp.einsum('bqhd,bqkhd->bqhk', q_b, k_s) * d ** -0.5
        logits = logits + bias.transpose(0, 1, 3, 2)
        logits = jnp.where(valid[:, :, None, :], logits, -jnp.inf)
        p = jax.nn.softmax(logits, axis=-1)
        return jnp.einsum('bqhk,bqkhd->bqhd', p, v_s)

    out = lax.map(block, jnp.arange(t // qb))
    return out.transpose(1, 0, 2, 3, 4).reshape(b, t, h, d)


def _gdn_branch(qkv_raw, z, b_raw, a_raw, conv_w, a_log, dt_bias, norm_g, s0, conv0):
    b, t, _ = qkv_raw.shape
    nh = z.shape[-1] // HEAD_DIM
    conv_out, conv_new = _causal_conv(qkv_raw, conv0, conv_w)
    act = jax.nn.silu(conv_out).reshape(b, t, 3, nh, HEAD_DIM)
    q = _l2norm(act[:, :, 0]) * HEAD_DIM ** -0.5
    k = _l2norm(act[:, :, 1])
    v = act[:, :, 2]
    beta = jax.nn.sigmoid(b_raw)
    g = -jnp.exp(a_log) * jax.nn.softplus(a_raw + dt_bias)
    o, s_new = _gated_delta_chunked(q, k, v, beta, g, s0)
    o = _rmsnorm(o, norm_g) * jax.nn.silu(z.reshape(b, t, nh, HEAD_DIM))
    return o.reshape(b, t, -1), s_new, conv_new


def _mlstm_branch(q_raw, k_raw, v_raw, o_raw, i_raw, f_raw, norm_g, c0, n0, m0):
    b, t, _ = q_raw.shape
    nh = q_raw.shape[-1] // HEAD_DIM
    shp = (b, t, nh, HEAD_DIM)
    q = q_raw.reshape(shp) * HEAD_DIM ** -0.5
    k = k_raw.reshape(shp)
    v = v_raw.reshape(shp)
    lf = jax.nn.log_sigmoid(f_raw)
    h, c_new, n_new, m_new = _mlstm_chunked(q, k, v, i_raw, lf, c0, n0, m0)
    h = _rmsnorm(h, norm_g) * jax.nn.sigmoid(o_raw.reshape(shp))
    return h.reshape(b, t, -1), c_new, n_new, m_new


def _segments(d_model):
    mw = d_model // 2
    nh = mw // HEAD_DIM
    widths = (3 * mw, mw, nh, nh, mw, mw, mw, mw, nh, nh, mw, mw, mw,
              IDX_HEADS * IDX_DIM, IDX_HEADS, IDX_DIM, N_BRANCH * d_model)
    offs = [0]
    for w in widths:
        offs.append(offs[-1] + w)
    names = ('g_qkv', 'g_z', 'g_b', 'g_a', 'm_q', 'm_k', 'm_v', 'm_o', 'm_i', 'm_f',
             'a_q', 'a_k', 'a_v', 'i_q', 'i_w', 'i_k', 'gates')
    return {nm: (offs[i], widths[i]) for i, nm in enumerate(names)}


_BIG = ('g_qkv', 'g_z', 'm_q', 'm_k', 'm_v', 'm_o', 'a_q', 'a_k', 'a_v', 'i_q')
_SMALL = ('g_b', 'g_a', 'm_i', 'm_f', 'i_w', 'i_k')


def _take_cols(a, seg, names):
    return jnp.concatenate([a[..., seg[n][0]:seg[n][0] + seg[n][1]] for n in names], axis=-1)


def _layout(seg, names):
    out, off = {}, 0
    for n in names:
        out[n] = (off, seg[n][1])
        off += seg[n][1]
    return out, off


def _trunk_layer(xs, l, P, rel_bias, sample_ctx, n_prompt, prompt_shape, sample_shape, init_sample):
    d = xs.shape[1]
    mw = d // 2
    nh = mw // HEAD_DIM
    bp, tp = prompt_shape
    bs, ts = sample_shape

    x1 = _ffn(xs, P['norm_g'][l, 0][None], P['ffa_w13'][l], P['ffa_w2'][l])
    g1 = P['norm_g'][l, 1][None]
    big = _inproj(x1, g1, P['w_big'][l], P['b_big'][l][None])
    small = _inproj_small(x1, g1, P['w_small'][l], P['b_small'][l][None])
    lb, ls = P['lay_big'], P['lay_small']

    def seg_b(rows, name):
        o, w = lb[name]
        return rows[:, o:o + w]

    def seg_s(rows, name):
        o, w = ls[name]
        return rows[:, o:o + w]

    outs = []
    states = []
    for grp in range(2):
        if grp == 0:
            rb, rs_, b, t = big[:n_prompt], small[:n_prompt], bp, tp
            init = (jnp.zeros((bp, nh, HEAD_DIM, HEAD_DIM), F32), jnp.zeros((bp, CONV_W - 1, 3 * mw), F32),
                    jnp.zeros((bp, nh, HEAD_DIM, HEAD_DIM), F32), jnp.zeros((bp, nh, HEAD_DIM), F32),
                    jnp.zeros((bp, nh), F32))
        else:
            rb, rs_, b, t = big[n_prompt:], small[n_prompt:], bs, ts
            init = tuple(a[l] for a in init_sample)
        s0, conv0, c0, n0, m0 = init

        def r3(a):
            return a.reshape(b, t, -1)

        o_a, s_new, conv_new = _gdn_branch(
            r3(seg_b(rb, 'g_qkv')), r3(seg_b(rb, 'g_z')), r3(seg_s(rs_, 'g_b')), r3(seg_s(rs_, 'g_a')),
            P['gdn_conv_w'][l], P['gdn_a_log'][l], P['gdn_dt_bias'][l], P['gdn_norm_g'][l], s0, conv0)
        o_b, c_new, n_new, m_new = _mlstm_branch(
            r3(seg_b(rb, 'm_q')), r3(seg_b(rb, 'm_k')), r3(seg_b(rb, 'm_v')), r3(seg_b(rb, 'm_o')),
            r3(seg_s(rs_, 'm_i')), r3(seg_s(rs_, 'm_f')), P['mlstm_norm_g'][l], c0, n0, m0)
        q_att = seg_b(rb, 'a_q').reshape(b, t, nh, HEAD_DIM)
        k_att = seg_b(rb, 'a_k').reshape(b, t, nh, HEAD_DIM)
        v_att = seg_b(rb, 'a_v').reshape(b, t, nh, HEAD_DIM)
        k_idx = _layernorm(r3(seg_s(rs_, 'i_k')), P['idx_ln_g'][l], P['idx_ln_b'][l])
        q_idx = seg_b(rb, 'i_q').reshape(b, t, IDX_HEADS, IDX_DIM)
        w_idx = r3(seg_s(rs_, 'i_w'))
        if grp == 0:
            o_c = _dsa_attend(q_att, q_idx, w_idx, k_idx, _local_gather(k_att, v_att), 0, rel_bias)
        else:
            cache_k, cache_v, cache_idx_k, page_table = sample_ctx
            past_len = page_table.shape[1] * PAGE_SIZE
            past_idx = cache_idx_k[l, page_table].reshape(b, past_len, IDX_DIM)
            k_idx_all = jnp.concatenate([past_idx, k_idx], axis=1)
            gather = _paged_gather(cache_k, cache_v, l, page_table, k_att, v_att)
            o_c = _dsa_attend(q_att, q_idx, w_idx, k_idx_all, gather, past_len, rel_bias)
        outs.append((o_a.reshape(b * t, mw), o_b.reshape(b * t, mw), o_c.reshape(b * t, mw)))
        states.append((k_att, v_att, k_idx, s_new, conv_new, c_new, n_new, m_new))

    oa = jnp.concatenate([outs[0][0], outs[1][0]], axis=0).astype(BF16)
    ob = jnp.concatenate([outs[0][1], outs[1][1]], axis=0).astype(BF16)
    oc = jnp.concatenate([outs[0][2], outs[1][2]], axis=0).astype(BF16)
    x2 = _merge(x1, g1, oa, ob, oc, P['w_gates'][l], P['b_gates'][l][None], P['w_branch'][l], P['w_out'][l])
    x3 = _ffn(x2, P['norm_g'][l, 2][None], P['ffb_w13'][l], P['ffb_w2'][l])
    return x3, states


def _prep_ffn(w13, w2, tf=512):
    f = w2.shape[1]
    fp = _round_up(f, tf)
    w1 = jnp.pad(w13[:, :, :f], ((0, 0), (0, 0), (0, fp - f)))
    w3 = jnp.pad(w13[:, :, f:], ((0, 0), (0, 0), (0, fp - f)))
    w13p = jnp.concatenate([w1, w3], axis=-1).astype(BF16)
    w2p = jnp.pad(w2, ((0, 0), (0, fp - f), (0, 0))).astype(BF16)
    return w13p, w2p


def kernel(x_prompt, x_sample, cache_k, cache_v, cache_idx_k, state_gdn, state_gdn_conv, state_mlstm_c,
           state_mlstm_n, state_mlstm_m, page_table, norm_g, final_g, ffa_w13, ffa_w2, ffb_w13, ffb_w2,
           w_in, b_in, gdn_conv_w, gdn_a_log, gdn_dt_bias, gdn_norm_g, mlstm_norm_g, idx_ln_g, idx_ln_b,
           rel_bias, w_branch, w_out):
    bp, tp, d = x_prompt.shape
    bs, ts, _ = x_sample.shape
    depth = w_in.shape[0]
    seg = _segments(d)
    lay_big, _ = _layout(seg, _BIG)
    lay_small, n_small = _layout(seg, _SMALL)
    pad_small = _round_up(n_small, LANE) - n_small

    ffa_w13p, ffa_w2p = _prep_ffn(ffa_w13, ffa_w2)
    ffb_w13p, ffb_w2p = _prep_ffn(ffb_w13, ffb_w2)
    go, gw = seg['gates']
    P = dict(
        norm_g=norm_g,
        ffa_w13=ffa_w13p, ffa_w2=ffa_w2p,
        ffb_w13=ffb_w13p, ffb_w2=ffb_w2p,
        w_big=_take_cols(w_in, seg, _BIG).astype(BF16), b_big=_take_cols(b_in, seg, _BIG),
        w_small=jnp.pad(_take_cols(w_in, seg, _SMALL), ((0, 0), (0, 0), (0, pad_small))),
        b_small=jnp.pad(_take_cols(b_in, seg, _SMALL), ((0, 0), (0, pad_small))),
        w_gates=w_in[:, :, go:go + gw].astype(BF16), b_gates=b_in[:, go:go + gw],
        w_branch=w_branch.astype(BF16), w_out=w_out.astype(BF16),
        gdn_conv_w=gdn_conv_w, gdn_a_log=gdn_a_log, gdn_dt_bias=gdn_dt_bias, gdn_norm_g=gdn_norm_g,
        mlstm_norm_g=mlstm_norm_g, idx_ln_g=idx_ln_g, idx_ln_b=idx_ln_b,
        lay_big=lay_big, lay_small=lay_small,
    )

    n_prompt = bp * tp
    xs = jnp.concatenate([x_prompt.reshape(n_prompt, d), x_sample.reshape(bs * ts, d)], axis=0)
    init_sample = (state_gdn, state_gdn_conv, state_mlstm_c, state_mlstm_n, state_mlstm_m)
    sample_ctx = (cache_k, cache_v, cache_idx_k, page_table)
    p_st, s_st = [], []
    for l in range(depth):
        xs, (p, s) = _trunk_layer(xs, l, P, rel_bias, sample_ctx, n_prompt, (bp, tp), (bs, ts), init_sample)
        p_st.append(p)
        s_st.append(s)
    y = _final_norm(xs, final_g[None])
    y_prompt = y[:n_prompt].reshape(bp, tp, d)
    y_sample = y[n_prompt:].reshape(bs, ts, d)

    def stacked(states, i):
        return jnp.stack([s[i] for s in states], axis=0)

    return (y_prompt, y_sample,
            *[stacked(p_st, i) for i in range(8)],
            *[stacked(s_st, i) for i in range(8)])
```

```python
import functools
import math

import jax
import jax.numpy as jnp
from jax import lax
from jax.experimental import pallas as pl
from jax.experimental.pallas import tpu as pltpu

F32 = jnp.float32
BF16 = jnp.bfloat16

HEAD_DIM = 128
CONV_W = 4
CHUNK = 64
IDX_HEADS = 16
IDX_DIM = 64
TOPK_MAX = 256
Q_BLOCK = 128
NUM_BUCKETS = 32
MAX_DISTANCE = 128
PAGE_SIZE = 128
N_BRANCH = 3
EPS = 1e-6

LANE = 128
VMEM_LIMIT = 56 * 1024 * 1024


def _round_up(a, b):
    return (a + b - 1) // b * b


def _rms_rows(x, g):
    return x * lax.rsqrt(jnp.mean(x * x, axis=-1, keepdims=True) + EPS) * g


def _ffn_kernel(x_ref, g_ref, w1_ref, w3_ref, w2_ref, o_ref, n_ref):
    j = pl.program_id(1)

    @pl.when(j == 0)
    def _():
        n_ref[...] = _rms_rows(x_ref[...], g_ref[...]).astype(BF16)
        o_ref[...] = jnp.zeros_like(o_ref)

    n = n_ref[...]
    g = jnp.dot(n, w1_ref[...], preferred_element_type=F32)
    u = jnp.dot(n, w3_ref[...], preferred_element_type=F32)
    a = (g * jax.nn.sigmoid(g) * u).astype(BF16)
    o_ref[...] += jnp.dot(a, w2_ref[...], preferred_element_type=F32)

    @pl.when(j == pl.num_programs(1) - 1)
    def _():
        o_ref[...] = x_ref[...] + 0.5 * o_ref[...]


def _ffn(x, g, w13p, w2p, *, tm=512, tf=512):
    m, d = x.shape
    fp = w2p.shape[0]
    nj = fp // tf
    return pl.pallas_call(
        _ffn_kernel,
        grid=(m // tm, nj),
        in_specs=[
            pl.BlockSpec((tm, d), lambda i, j: (i, 0)),
            pl.BlockSpec((1, d), lambda i, j: (0, 0)),
            pl.BlockSpec((d, tf), lambda i, j: (0, j)),
            pl.BlockSpec((d, tf), lambda i, j: (0, j + nj)),
            pl.BlockSpec((tf, d), lambda i, j: (j, 0)),
        ],
        out_specs=pl.BlockSpec((tm, d), lambda i, j: (i, 0)),
        out_shape=jax.ShapeDtypeStruct((m, d), F32),
        scratch_shapes=[pltpu.VMEM((tm, d), BF16)],
        compiler_params=pltpu.CompilerParams(
            dimension_semantics=("parallel", "arbitrary"), vmem_limit_bytes=VMEM_LIMIT),
        name="ffn",
    )(x, g, w13p, w13p, w2p)


def _inproj_kernel(x_ref, g_ref, w_ref, b_ref, o_ref, n_ref):
    @pl.when(pl.program_id(1) == 0)
    def _():
        n_ref[...] = _rms_rows(x_ref[...], g_ref[...]).astype(BF16)

    o_ref[...] = jnp.dot(n_ref[...], w_ref[...], preferred_element_type=F32) + b_ref[...]


def _inproj(x, g, w, b, *, tm=512, tn=1024):
    m, d = x.shape
    n = w.shape[1]
    return pl.pallas_call(
        _inproj_kernel,
        grid=(m // tm, n // tn),
        in_specs=[
            pl.BlockSpec((tm, d), lambda i, j: (i, 0)),
            pl.BlockSpec((1, d), lambda i, j: (0, 0)),
            pl.BlockSpec((d, tn), lambda i, j: (0, j)),
            pl.BlockSpec((1, tn), lambda i, j: (0, j)),
        ],
        out_specs=pl.BlockSpec((tm, tn), lambda i, j: (i, j)),
        out_shape=jax.ShapeDtypeStruct((m, n), F32),
        scratch_shapes=[pltpu.VMEM((tm, d), BF16)],
        compiler_params=pltpu.CompilerParams(
            dimension_semantics=("parallel", "arbitrary"), vmem_limit_bytes=VMEM_LIMIT),
        name="inproj",
    )(x, g, w, b)


def _inproj_small_kernel(x_ref, g_ref, w_ref, b_ref, lg_ref, lb_ref, o_ref, ki_ref):
    n = _rms_rows(x_ref[...], g_ref[...])
    o = jnp.dot(n, w_ref[...], preferred_element_type=F32, precision=lax.Precision.HIGHEST) + b_ref[...]
    o_ref[...] = o
    ik = o[:, :IDX_DIM]
    mu = jnp.mean(ik, axis=-1, keepdims=True)
    var = jnp.mean(jnp.square(ik - mu), axis=-1, keepdims=True)
    ki_ref[...] = (ik - mu) * lax.rsqrt(var + EPS) * lg_ref[...] + lb_ref[...]


def _inproj_small(x, g, w, b, ln_g, ln_b, *, tm=512):
    m, d = x.shape
    n = w.shape[1]
    return pl.pallas_call(
        _inproj_small_kernel,
        grid=(m // tm,),
        in_specs=[
            pl.BlockSpec((tm, d), lambda i: (i, 0)),
            pl.BlockSpec((1, d), lambda i: (0, 0)),
            pl.BlockSpec((d, n), lambda i: (0, 0)),
            pl.BlockSpec((1, n), lambda i: (0, 0)),
            pl.BlockSpec((1, IDX_DIM), lambda i: (0, 0)),
            pl.BlockSpec((1, IDX_DIM), lambda i: (0, 0)),
        ],
        out_specs=[pl.BlockSpec((tm, n), lambda i: (i, 0)), pl.BlockSpec((tm, IDX_DIM), lambda i: (i, 0))],
        out_shape=[jax.ShapeDtypeStruct((m, n), F32), jax.ShapeDtypeStruct((m, IDX_DIM), F32)],
        compiler_params=pltpu.CompilerParams(
            dimension_semantics=("parallel",), vmem_limit_bytes=VMEM_LIMIT),
        name="inproj_small",
    )(x, g, w, b, ln_g, ln_b)


def _merge_kernel(x_ref, g_ref, oa_ref, ob_ref, oc_ref, wg0_ref, wg1_ref, wg2_ref,
                  bg0_ref, bg1_ref, bg2_ref, wb0_ref, wb1_ref, wb2_ref, wo_ref, o_ref, n_ref):
    j = pl.program_id(1)

    @pl.when(j == 0)
    def _():
        n_ref[...] = _rms_rows(x_ref[...], g_ref[...]).astype(BF16)
        o_ref[...] = jnp.zeros_like(o_ref)

    n = n_ref[...]
    merged = None
    for o_r, wg_r, bg_r, wb_r in ((oa_ref, wg0_ref, bg0_ref, wb0_ref),
                                  (ob_ref, wg1_ref, bg1_ref, wb1_ref),
                                  (oc_ref, wg2_ref, bg2_ref, wb2_ref)):
        gate = jax.nn.sigmoid(jnp.dot(n, wg_r[...], preferred_element_type=F32) + bg_r[...])
        y = jnp.dot(o_r[...], wb_r[0], preferred_element_type=F32)
        merged = gate * y if merged is None else merged + gate * y
    o_ref[...] += jnp.dot(merged.astype(BF16), wo_ref[...], preferred_element_type=F32)

    @pl.when(j == pl.num_programs(1) - 1)
    def _():
        o_ref[...] = x_ref[...] + o_ref[...]


def _merge(x, g, oa, ob, oc, wg, bg, wb, wo, *, tm=512, tn=512):
    m, d = x.shape
    w = oa.shape[1]
    nj = d // tn
    wg_specs = [pl.BlockSpec((d, tn), functools.partial(lambda i, j, n: (0, n * nj + j), n=n))
                for n in range(N_BRANCH)]
    bg_specs = [pl.BlockSpec((1, tn), functools.partial(lambda i, j, n: (0, n * nj + j), n=n))
                for n in range(N_BRANCH)]
    wb_specs = [pl.BlockSpec((1, w, tn), functools.partial(lambda i, j, n: (n, 0, j), n=n))
                for n in range(N_BRANCH)]
    o_spec = pl.BlockSpec((tm, w), lambda i, j: (i, 0))
    return pl.pallas_call(
        _merge_kernel,
        grid=(m // tm, nj),
        in_specs=[pl.BlockSpec((tm, d), lambda i, j: (i, 0)),
                  pl.BlockSpec((1, d), lambda i, j: (0, 0)),
                  o_spec, o_spec, o_spec,
                  *wg_specs, *bg_specs, *wb_specs,
                  pl.BlockSpec((tn, d), lambda i, j: (j, 0))],
        out_specs=pl.BlockSpec((tm, d), lambda i, j: (i, 0)),
        out_shape=jax.ShapeDtypeStruct((m, d), F32),
        scratch_shapes=[pltpu.VMEM((tm, d), BF16)],
        compiler_params=pltpu.CompilerParams(
            dimension_semantics=("parallel", "arbitrary"), vmem_limit_bytes=VMEM_LIMIT),
        name="merge",
    )(x, g, oa, ob, oc, wg, wg, wg, bg, bg, bg, wb, wb, wb, wo)


def _norm_kernel(x_ref, g_ref, o_ref):
    o_ref[...] = _rms_rows(x_ref[...], g_ref[...])


def _final_norm(x, g, *, tm=512):
    m, d = x.shape
    return pl.pallas_call(
        _norm_kernel,
        grid=(m // tm,),
        in_specs=[pl.BlockSpec((tm, d), lambda i: (i, 0)), pl.BlockSpec((1, d), lambda i: (0, 0))],
        out_specs=pl.BlockSpec((tm, d), lambda i: (i, 0)),
        out_shape=jax.ShapeDtypeStruct((m, d), F32),
        compiler_params=pltpu.CompilerParams(dimension_semantics=("parallel",)),
        name="final_norm",
    )(x, g)


NEG = -1e30
INT_MIN = -2 ** 31


def _order_key(x):
    bits = lax.bitcast_convert_type(x, jnp.int32)
    return bits ^ ((bits >> 31) & jnp.int32(0x7FFFFFFF))


def _kth_largest_key(count_ge, topk, shape):
    t0 = jnp.where(count_ge(jnp.zeros(shape, jnp.int32)) >= topk, jnp.int32(0), jnp.int32(INT_MIN))

    def bit(b, t):
        cand = t + lax.shift_left(jnp.int32(1), jnp.int32(30) - b)
        return jnp.where(count_ge(cand) >= topk, cand, t)

    return lax.fori_loop(0, 31, bit, t0)


def _softmax_update(s, vb, m_prev, l_prev, acc_prev):
    reps = s.shape[1] // LANE
    m_new = jnp.maximum(m_prev, jnp.max(s, axis=1, keepdims=True))
    alpha = jnp.exp(m_prev - m_new)
    p = jnp.exp(s - jnp.concatenate([m_new] * reps, axis=1))
    l_new = alpha * l_prev + jnp.sum(p, axis=1, keepdims=True)
    pv = jnp.dot(p.astype(BF16), vb, preferred_element_type=F32)
    dreps = pv.shape[1] // LANE
    acc_new = jnp.concatenate([alpha] * dreps, axis=1) * acc_prev + pv
    return m_new, l_new, acc_new


def _dsa_prompt_kernel(far_ref, qi_ref, w_ref, ki_ref, q_ref, k_ref, v_ref, bd_ref, bs_ref, o_ref,
                       key_ref, thr_ref, wb_ref, qb_ref, m_ref, l_ref, acc_ref, *, topk, tq, nh):
    i = pl.program_id(0)
    j = pl.program_id(1)
    rg = 128

    @pl.when(j == 0)
    def _select():
        for h in range(IDX_HEADS):
            wb_ref[h] = jnp.broadcast_to(w_ref[:, h:h + 1], (tq, tq))
        qb_ref[...] = (q_ref[...] * HEAD_DIM ** -0.5).astype(BF16)
        m_ref[...] = jnp.full_like(m_ref, NEG)
        l_ref[...] = jnp.zeros_like(l_ref)
        acc_ref[...] = jnp.zeros_like(acc_ref)
        qi = qi_ref[...].reshape(IDX_HEADS * tq, IDX_DIM)
        row = lax.broadcasted_iota(jnp.int32, (tq, tq), 0)
        col = lax.broadcasted_iota(jnp.int32, (tq, tq), 1)

        def chunk(c, carry):
            kc = ki_ref[pl.ds(pl.multiple_of(c * tq, tq), tq), :]
            s = lax.dot_general(qi, kc, (((1,), (1,)), ((), ())), preferred_element_type=F32)
            sc = jnp.zeros((tq, tq), F32)
            for h in range(IDX_HEADS):
                sc = sc + jnp.maximum(s[h * tq:(h + 1) * tq], 0.0) * wb_ref[h]
            sc = jnp.where(col <= row + jnp.where(c < i, tq, 0), sc, -jnp.inf)
            key_ref[c] = _order_key(sc)
            return carry

        lax.fori_loop(0, i + 1, chunk, 0)

        for r in range(tq // rg):
            rows = pl.ds(r * rg, rg)

            def count_ge(cand):
                def body(c, a):
                    hit = jnp.where(key_ref[c, rows, :] >= cand, 1.0, 0.0)
                    for x in range(tq // LANE):
                        a = a + hit[:, x * LANE:(x + 1) * LANE]
                    return a
                a = lax.fori_loop(0, i + 1, body, jnp.zeros((rg, LANE), F32))
                return jnp.sum(a, axis=1, keepdims=True)

            t = _kth_largest_key(count_ge, float(topk), (rg, 1))
            thr_ref[rows, :] = jnp.broadcast_to(t, (rg, LANE))

    def attend(bias_of_head):
        kk = key_ref[j]
        thr = thr_ref[...]
        selb = jnp.concatenate(
            [jnp.where(kk[:, x * LANE:(x + 1) * LANE] >= thr, 0.0, NEG) for x in range(tq // LANE)], axis=1)
        kb = k_ref[...].astype(BF16)
        vb = v_ref[...].astype(BF16)
        for h in range(nh):
            hs = slice(h * HEAD_DIM, (h + 1) * HEAD_DIM)
            s = lax.dot_general(qb_ref[:, hs], kb[:, hs], (((1,), (1,)), ((), ())),
                                preferred_element_type=F32)
            s = s + selb + bias_of_head(h)
            m_new, l_new, acc_new = _softmax_update(s, vb[:, hs], m_ref[h], l_ref[h], acc_ref[:, hs])
            m_ref[h] = m_new
            l_ref[h] = l_new
            acc_ref[:, hs] = acc_new

    @pl.when(j < i - 1)
    def _far():
        attend(lambda h: far_ref[h])

    @pl.when(j == i - 1)
    def _sub():
        attend(lambda h: bs_ref[h])

    @pl.when(j == i)
    def _diag():
        attend(lambda h: bd_ref[h])
        for h in range(nh):
            hs = slice(h * HEAD_DIM, (h + 1) * HEAD_DIM)
            o_ref[:, hs] = (acc_ref[:, hs] / l_ref[h]).astype(BF16)


def _t5_bucket(dist):
    n = jnp.maximum(dist, 0)
    max_exact = NUM_BUCKETS // 2
    nf = jnp.maximum(n, 1).astype(F32)
    large = max_exact + (jnp.log(nf / max_exact) / math.log(MAX_DISTANCE / max_exact)
                         * (NUM_BUCKETS - max_exact)).astype(jnp.int32)
    large = jnp.minimum(large, NUM_BUCKETS - 1)
    return jnp.where(n < max_exact, n, large)


def _dsa_prompt(big, row0, t, cols, qi, w, ki, rel_bias, *, tq=256):
    mw = rel_bias.shape[1] * HEAD_DIM
    nh = rel_bias.shape[1]
    assert t % tq == 0 and row0 % tq == 0 and tq >= MAX_DISTANCE
    nq = t // tq
    rb = row0 // tq
    topk = min(TOPK_MAX, t // 4)
    r = jnp.arange(tq)[:, None]
    c = jnp.arange(tq)[None, :]
    bd = jnp.where((r >= c)[None], jnp.moveaxis(rel_bias[_t5_bucket(r - c)], -1, 0), NEG)
    bs = jnp.moveaxis(rel_bias[_t5_bucket(tq + r - c)], -1, 0)
    far = rel_bias[_t5_bucket(jnp.int32(tq + 1))]
    cq, ck, cv = cols
    kern = functools.partial(_dsa_prompt_kernel, topk=topk, tq=tq, nh=nh)
    return pl.pallas_call(
        kern,
        grid=(nq, nq),
        in_specs=[
            pl.BlockSpec(memory_space=pltpu.SMEM),
            pl.BlockSpec((IDX_HEADS, tq, IDX_DIM), lambda i, j: (0, i, 0)),
            pl.BlockSpec((tq, IDX_HEADS), lambda i, j: (i, 0)),
            pl.BlockSpec((t, IDX_DIM), lambda i, j: (0, 0)),
            pl.BlockSpec((tq, mw), lambda i, j: (rb + i, cq)),
            pl.BlockSpec((tq, mw), lambda i, j: (rb + jnp.minimum(j, i), ck)),
            pl.BlockSpec((tq, mw), lambda i, j: (rb + jnp.minimum(j, i), cv)),
            pl.BlockSpec((nh, tq, tq), lambda i, j: (0, 0, 0)),
            pl.BlockSpec((nh, tq, tq), lambda i, j: (0, 0, 0)),
        ],
        out_specs=pl.BlockSpec((tq, mw), lambda i, j: (i, 0)),
        out_shape=jax.ShapeDtypeStruct((t, mw), BF16),
        scratch_shapes=[
            pltpu.VMEM((nq, tq, tq), jnp.int32),
            pltpu.VMEM((tq, LANE), jnp.int32),
            pltpu.VMEM((IDX_HEADS, tq, tq), F32),
            pltpu.VMEM((tq, mw), BF16),
            pltpu.VMEM((nh, tq, LANE), F32),
            pltpu.VMEM((nh, tq, LANE), F32),
            pltpu.VMEM((tq, mw), F32),
        ],
        compiler_params=pltpu.CompilerParams(
            dimension_semantics=("parallel", "arbitrary"), vmem_limit_bytes=VMEM_LIMIT),
        name="dsa_prompt",
    )(far, qi, w, ki, big, big, big, bd, bs)


def _dsa_sample_kernel(pt_ref, qi_ref, w_ref, kin_ref, q_ref, kn_ref, vn_ref, bias_ref, biasn_ref, *rest,
                       topk, t, n_pages, g_pages, nh, past):
    idx_refs = rest[:n_pages]
    kp_refs = rest[n_pages:n_pages + g_pages]
    vp_refs = rest[n_pages + g_pages:n_pages + 2 * g_pages]
    o_ref, selb_ref, selbn_ref, qbd_ref, m_ref, l_ref, acc_ref = rest[n_pages + 2 * g_pages:]
    g = pl.program_id(1)
    ng = n_pages // g_pages
    gw = g_pages * PAGE_SIZE
    rows = nh * t
    mw = nh * HEAD_DIM

    @pl.when(g == 0)
    def _select():
        qi = qi_ref[...]
        wcol = w_ref[...]
        pad = jnp.zeros((PAGE_SIZE - t, IDX_DIM), F32)
        keys = jnp.concatenate([r[...] for r in idx_refs] + [kin_ref[...], pad], axis=0).astype(BF16)
        s = lax.dot_general(qi, keys, (((1,), (1,)), ((), ())), preferred_element_type=F32)
        s = jnp.maximum(s, 0.0) * wcol
        sc = s[0:t]
        for h in range(1, IDX_HEADS):
            sc = sc + s[h * t:(h + 1) * t]
        kpos = lax.broadcasted_iota(jnp.int32, sc.shape, 1)
        qrow = lax.broadcasted_iota(jnp.int32, sc.shape, 0)
        key = _order_key(jnp.where(kpos <= past + qrow, sc, -jnp.inf))

        def count_ge(cand):
            return jnp.sum(jnp.where(key >= cand, 1.0, 0.0), axis=1, keepdims=True)

        thr = _kth_largest_key(count_ge, float(topk), (t, 1))
        selb = jnp.where(key >= thr, 0.0, NEG)
        for x in range(ng):
            selb_ref[x] = selb[:, x * gw:(x + 1) * gw]
        selbn_ref[...] = selb[:, past:]

        qs = jnp.concatenate([q_ref[...] * HEAD_DIM ** -0.5] * nh, axis=0)
        rr = lax.broadcasted_iota(jnp.int32, (rows, mw), 0) >> (t.bit_length() - 1)
        cc = lax.broadcasted_iota(jnp.int32, (rows, mw), 1) >> (HEAD_DIM.bit_length() - 1)
        qbd_ref[...] = jnp.where(rr == cc, qs, 0.0).astype(BF16)
        m_ref[...] = jnp.full_like(m_ref, NEG)
        l_ref[...] = jnp.zeros_like(l_ref)
        acc_ref[...] = jnp.zeros_like(acc_ref)

    def update(kb, vb, bias):
        s = lax.dot_general(qbd_ref[...], kb, (((1,), (1,)), ((), ())), preferred_element_type=F32)
        m_new, l_new, acc_new = _softmax_update(s + bias, vb, m_ref[...], l_ref[...], acc_ref[...])
        m_ref[...] = m_new
        l_ref[...] = l_new
        acc_ref[...] = acc_new

    kb = jnp.concatenate([r[...] for r in kp_refs], axis=0).astype(BF16)
    vb = jnp.concatenate([r[...] for r in vp_refs], axis=0).astype(BF16)
    update(kb, vb, bias_ref[g] + jnp.concatenate([selb_ref[g]] * nh, axis=0))

    @pl.when(g == ng - 1)
    def _finish():
        padn = jnp.zeros((PAGE_SIZE - t, mw), F32)
        kn = jnp.concatenate([kn_ref[...], padn], axis=0).astype(BF16)
        vn = jnp.concatenate([vn_ref[...], padn], axis=0).astype(BF16)
        update(kn, vn, biasn_ref[...] + jnp.concatenate([selbn_ref[...]] * nh, axis=0))
        for h in range(nh):
            hs = slice(h * HEAD_DIM, (h + 1) * HEAD_DIM)
            rs = slice(h * t, (h + 1) * t)
            o_ref[:, hs] = (acc_ref[rs, hs] / l_ref[rs, :]).astype(BF16)


def _dsa_sample(big, row0, b, t, cols, qi, w, kin, cache_k, cache_v, cache_idx_k, l, page_table, rel_bias):
    nh = rel_bias.shape[1]
    mw = nh * HEAD_DIM
    n_pages = page_table.shape[1]
    past = n_pages * PAGE_SIZE
    g_pages = math.gcd(n_pages, 4)
    ng = n_pages // g_pages
    gw = g_pages * PAGE_SIZE
    topk = min(TOPK_MAX, (past + t) // 4)
    assert row0 % t == 0 and t % 8 == 0 and t <= PAGE_SIZE and t & (t - 1) == 0
    rb = row0 // t
    rows = nh * t
    qpos = past + jnp.arange(t)
    kpos = jnp.arange(past + PAGE_SIZE)
    dist = qpos[:, None] - kpos[None, :]
    bias = jnp.where((dist >= 0)[None], jnp.moveaxis(rel_bias[_t5_bucket(dist)], -1, 0), NEG)
    bias = bias.reshape(rows, past + PAGE_SIZE)
    bias_past = bias[:, :past].reshape(rows, ng, gw).transpose(1, 0, 2)
    bias_new = bias[:, past:]
    cq, ck, cv = cols

    def page_map(x, bi, gi, pt):
        return (l, pt[bi, gi * g_pages + x], 0, 0)

    def idx_map(x, bi, gi, pt):
        return (l, pt[bi, x], 0, 0)

    in_specs = [
        pl.BlockSpec((None, IDX_HEADS * t, IDX_DIM), lambda bi, gi, pt: (bi, 0, 0)),
        pl.BlockSpec((None, IDX_HEADS * t, 1), lambda bi, gi, pt: (bi, 0, 0)),
        pl.BlockSpec((None, t, IDX_DIM), lambda bi, gi, pt: (bi, 0, 0)),
        pl.BlockSpec((t, mw), lambda bi, gi, pt: (rb + bi, cq)),
        pl.BlockSpec((t, mw), lambda bi, gi, pt: (rb + bi, ck)),
        pl.BlockSpec((t, mw), lambda bi, gi, pt: (rb + bi, cv)),
        pl.BlockSpec((ng, rows, gw), lambda bi, gi, pt: (0, 0, 0)),
        pl.BlockSpec((rows, PAGE_SIZE), lambda bi, gi, pt: (0, 0)),
    ]
    in_specs += [pl.BlockSpec((None, None, PAGE_SIZE, IDX_DIM), functools.partial(idx_map, x))
                 for x in range(n_pages)]
    in_specs += [pl.BlockSpec((None, None, PAGE_SIZE, mw), functools.partial(page_map, x))
                 for x in range(g_pages)]
    in_specs += [pl.BlockSpec((None, None, PAGE_SIZE, mw), functools.partial(page_map, x))
                 for x in range(g_pages)]
    kern = functools.partial(_dsa_sample_kernel, topk=topk, t=t, n_pages=n_pages, g_pages=g_pages,
                             nh=nh, past=past)
    grid_spec = pltpu.PrefetchScalarGridSpec(
        num_scalar_prefetch=1,
        grid=(b, ng),
        in_specs=in_specs,
        out_specs=pl.BlockSpec((t, mw), lambda bi, gi, pt: (bi, 0)),
        scratch_shapes=[
            pltpu.VMEM((ng, t, gw), F32),
            pltpu.VMEM((t, PAGE_SIZE), F32),
            pltpu.VMEM((rows, mw), BF16),
            pltpu.VMEM((rows, LANE), F32),
            pltpu.VMEM((rows, LANE), F32),
            pltpu.VMEM((rows, mw), F32),
        ],
    )
    return pl.pallas_call(
        kern,
        grid_spec=grid_spec,
        out_shape=jax.ShapeDtypeStruct((b * t, mw), BF16),
        compiler_params=pltpu.CompilerParams(
            dimension_semantics=("parallel", "arbitrary"), vmem_limit_bytes=VMEM_LIMIT),
        name="dsa_sample",
    )(page_table, qi, w, kin, big, big, big, bias_past, bias_new,
      *([cache_idx_k] * n_pages), *([cache_k] * g_pages), *([cache_v] * g_pages))


def _l2norm(x):
    return x * lax.rsqrt(jnp.sum(x * x, axis=-1, keepdims=True) + EPS)


def _rmsnorm(x, g):
    return x * lax.rsqrt(jnp.mean(x * x, axis=-1, keepdims=True) + EPS) * g


def _layernorm(x, g, b):
    mu = jnp.mean(x, axis=-1, keepdims=True)
    var = jnp.mean(jnp.square(x - mu), axis=-1, keepdims=True)
    return (x - mu) * lax.rsqrt(var + EPS) * g + b


def _causal_conv(u, state, w):
    t = u.shape[1]
    xp = jnp.concatenate([state, u], axis=1)
    y = xp[:, 0:t] * w[0]
    for j in range(1, CONV_W):
        y = y + xp[:, j:j + t] * w[j]
    return y, xp[:, t:]


def _to_chunks4(x, c):
    b, t, h, d = x.shape
    return x.reshape(b, t // c, c, h, d).transpose(1, 0, 3, 2, 4)


def _to_chunks3(x, c):
    b, t, h = x.shape
    return x.reshape(b, t // c, c, h).transpose(1, 0, 3, 2)


def _from_chunks4(o):
    n, b, h, c, d = o.shape
    return o.transpose(1, 0, 3, 2, 4).reshape(b, n * c, h, d)


def _gated_delta_chunked(q, k, v, beta, g, s0):
    t = q.shape[1]
    c = math.gcd(t, CHUNK)
    qc, kc, vc = _to_chunks4(q, c), _to_chunks4(k, c), _to_chunks4(v, c)
    bc, gc = _to_chunks3(beta, c), _to_chunks3(g, c)
    gcum = jnp.cumsum(gc, axis=-1)
    idx = jnp.arange(c)
    incl = idx[:, None] >= idx[None, :]
    strict = idx[:, None] > idx[None, :]
    diff = gcum[..., :, None] - gcum[..., None, :]
    decay = jnp.where(incl, jnp.exp(jnp.where(incl, diff, 0.0)), 0.0)
    kb = kc * bc[..., None]
    a_mat = jnp.where(strict, jnp.einsum('nbhid,nbhjd->nbhij', kb, kc) * decay, 0.0) + jnp.eye(c, dtype=q.dtype)
    u = lax.linalg.triangular_solve(a_mat, vc * bc[..., None], left_side=True, lower=True, unit_diagonal=True)
    w = lax.linalg.triangular_solve(a_mat, kb * jnp.exp(gcum)[..., None], left_side=True, lower=True,
                                    unit_diagonal=True)
    qk = jnp.einsum('nbhid,nbhjd->nbhij', qc, kc) * decay
    q_dec = qc * jnp.exp(gcum)[..., None]
    k_dec = kc * jnp.exp(gcum[..., -1:] - gcum)[..., None]
    g_last = jnp.exp(gcum[..., -1])

    def step(s, xs):
        u_n, w_n, qk_n, qd_n, kd_n, gl_n = xs
        v_new = u_n - jnp.einsum('bhcd,bhde->bhce', w_n, s)
        o = jnp.einsum('bhcd,bhde->bhce', qd_n, s) + jnp.einsum('bhij,bhje->bhie', qk_n, v_new)
        s = s * gl_n[..., None, None] + jnp.einsum('bhcd,bhce->bhde', kd_n, v_new)
        return s, o

    s, o = lax.scan(step, s0, (u, w, qk, q_dec, k_dec, g_last))
    return _from_chunks4(o), s


def _mlstm_chunked(q, k, v, ig, lf, c0, n0, m0):
    t = q.shape[1]
    c = math.gcd(t, CHUNK)
    qc, kc, vc = _to_chunks4(q, c), _to_chunks4(k, c), _to_chunks4(v, c)
    igc, lfc = _to_chunks3(ig, c), _to_chunks3(lf, c)
    bcum = jnp.cumsum(lfc, axis=-1)
    idx = jnp.arange(c)
    incl = idx[:, None] >= idx[None, :]
    dlog = jnp.where(incl, bcum[..., :, None] - bcum[..., None, :] + igc[..., None, :], -jnp.inf)
    dmax = jnp.max(dlog, axis=-1)
    a_end = bcum[..., -1:] - bcum + igc
    a_max = jnp.max(a_end, axis=-1)
    qk = jnp.einsum('nbhid,nbhjd->nbhij', qc, kc)

    def step(carry, xs):
        cs, ns, ms = carry
        qn, kn, vn, bn, dl, dm, qk_n, an, am = xs
        inter = bn + ms[..., None]
        mt = jnp.maximum(inter, dm)
        sc = qk_n * jnp.exp(dl - mt[..., None])
        w_inter = jnp.exp(inter - mt)
        num = w_inter[..., None] * jnp.einsum('bhcd,bhde->bhce', qn, cs) + jnp.einsum('bhij,bhje->bhie', sc, vn)
        den = w_inter * jnp.einsum('bhcd,bhd->bhc', qn, ns) + jnp.sum(sc, axis=-1)
        h = num / jnp.maximum(jnp.abs(den), jnp.exp(-mt))[..., None]
        m_new = jnp.maximum(bn[..., -1] + ms, am)
        keep = jnp.exp(bn[..., -1] + ms - m_new)
        kw = kn * jnp.exp(an - m_new[..., None])[..., None]
        cs = keep[..., None, None] * cs + jnp.einsum('bhcd,bhce->bhde', kw, vn)
        ns = keep[..., None] * ns + jnp.sum(kw, axis=-2)
        return (cs, ns, m_new), h

    (cs, ns, ms), h = lax.scan(step, (c0, n0, m0), (qc, kc, vc, bcum, dlog, dmax, qk, a_end, a_max))
    return _from_chunks4(h), cs, ns, ms


def _gdn_branch(qkv_raw, z, b_raw, a_raw, conv_w, a_log, dt_bias, norm_g, s0, conv0):
    b, t, _ = qkv_raw.shape
    nh = z.shape[-1] // HEAD_DIM
    conv_out, conv_new = _causal_conv(qkv_raw, conv0, conv_w)
    act = jax.nn.silu(conv_out).reshape(b, t, 3, nh, HEAD_DIM)
    q = _l2norm(act[:, :, 0]) * HEAD_DIM ** -0.5
    k = _l2norm(act[:, :, 1])
    v = act[:, :, 2]
    beta = jax.nn.sigmoid(b_raw)
    g = -jnp.exp(a_log) * jax.nn.softplus(a_raw + dt_bias)
    o, s_new = _gated_delta_chunked(q, k, v, beta, g, s0)
    o = _rmsnorm(o, norm_g) * jax.nn.silu(z.reshape(b, t, nh, HEAD_DIM))
    return o.reshape(b, t, -1), s_new, conv_new


def _mlstm_branch(q_raw, k_raw, v_raw, o_raw, i_raw, f_raw, norm_g, c0, n0, m0):
    b, t, _ = q_raw.shape
    nh = q_raw.shape[-1] // HEAD_DIM
    shp = (b, t, nh, HEAD_DIM)
    q = q_raw.reshape(shp) * HEAD_DIM ** -0.5
    k = k_raw.reshape(shp)
    v = v_raw.reshape(shp)
    lf = jax.nn.log_sigmoid(f_raw)
    h, c_new, n_new, m_new = _mlstm_chunked(q, k, v, i_raw, lf, c0, n0, m0)
    h = _rmsnorm(h, norm_g) * jax.nn.sigmoid(o_raw.reshape(shp))
    return h.reshape(b, t, -1), c_new, n_new, m_new


def _segments(d_model):
    mw = d_model // 2
    nh = mw // HEAD_DIM
    widths = (3 * mw, mw, nh, nh, mw, mw, mw, mw, nh, nh, mw, mw, mw,
              IDX_HEADS * IDX_DIM, IDX_HEADS, IDX_DIM, N_BRANCH * d_model)
    offs = [0]
    for w in widths:
        offs.append(offs[-1] + w)
    names = ('g_qkv', 'g_z', 'g_b', 'g_a', 'm_q', 'm_k', 'm_v', 'm_o', 'm_i', 'm_f',
             'a_q', 'a_k', 'a_v', 'i_q', 'i_w', 'i_k', 'gates')
    return {nm: (offs[i], widths[i]) for i, nm in enumerate(names)}


_BIG = ('g_qkv', 'g_z', 'm_q', 'm_k', 'm_v', 'm_o', 'a_q', 'a_k', 'a_v', 'i_q')
_SMALL = ('i_k', 'g_b', 'g_a', 'm_i', 'm_f', 'i_w')


def _take_cols(a, seg, names):
    return jnp.concatenate([a[..., seg[n][0]:seg[n][0] + seg[n][1]] for n in names], axis=-1)


def _layout(seg, names):
    out, off = {}, 0
    for n in names:
        out[n] = (off, seg[n][1])
        off += seg[n][1]
    return out, off


def _trunk_layer(xs, l, P, rel_bias, sample_ctx, n_prompt, prompt_shape, sample_shape, init_sample):
    d = xs.shape[1]
    mw = d // 2
    nh = mw // HEAD_DIM
    bp, tp = prompt_shape
    bs, ts = sample_shape

    x1 = _ffn(xs, P['norm_g'][l, 0][None], P['ffa_w13'][l], P['ffa_w2'][l])
    g1 = P['norm_g'][l, 1][None]
    big = _inproj(x1, g1, P['w_big'][l], P['b_big'][l][None])
    small, kidx = _inproj_small(x1, g1, P['w_small'][l], P['b_small'][l][None],
                                P['idx_ln_g'][l][None], P['idx_ln_b'][l][None])
    lb, ls = P['lay_big'], P['lay_small']
    att_cols = tuple(lb[n][0] // mw for n in ('a_q', 'a_k', 'a_v'))
    idx_scale = IDX_DIM ** -0.5 * IDX_HEADS ** -0.5

    def seg_b(rows, name):
        o, w = lb[name]
        return rows[:, o:o + w]

    def seg_s(rows, name):
        o, w = ls[name]
        return rows[:, o:o + w]

    outs = []
    states = []
    for grp in range(2):
        if grp == 0:
            rb, rs_, b, t = big[:n_prompt], small[:n_prompt], bp, tp
            init = (jnp.zeros((bp, nh, HEAD_DIM, HEAD_DIM), F32), jnp.zeros((bp, CONV_W - 1, 3 * mw), F32),
                    jnp.zeros((bp, nh, HEAD_DIM, HEAD_DIM), F32), jnp.zeros((bp, nh, HEAD_DIM), F32),
                    jnp.zeros((bp, nh), F32))
        else:
            rb, rs_, b, t = big[n_prompt:], small[n_prompt:], bs, ts
            init = tuple(a[l] for a in init_sample)
        s0, conv0, c0, n0, m0 = init

        def r3(a):
            return a.reshape(b, t, -1)

        o_a, s_new, conv_new = _gdn_branch(
            r3(seg_b(rb, 'g_qkv')), r3(seg_b(rb, 'g_z')), r3(seg_s(rs_, 'g_b')), r3(seg_s(rs_, 'g_a')),
            P['gdn_conv_w'][l], P['gdn_a_log'][l], P['gdn_dt_bias'][l], P['gdn_norm_g'][l], s0, conv0)
        o_b, c_new, n_new, m_new = _mlstm_branch(
            r3(seg_b(rb, 'm_q')), r3(seg_b(rb, 'm_k')), r3(seg_b(rb, 'm_v')), r3(seg_b(rb, 'm_o')),
            r3(seg_s(rs_, 'm_i')), r3(seg_s(rs_, 'm_f')), P['mlstm_norm_g'][l], c0, n0, m0)
        k_att = seg_b(rb, 'a_k').reshape(b, t, nh, HEAD_DIM)
        v_att = seg_b(rb, 'a_v').reshape(b, t, nh, HEAD_DIM)
        row0 = 0 if grp == 0 else n_prompt
        k_idx = kidx[row0:row0 + b * t].reshape(b, t, IDX_DIM)
        q_idx = seg_b(rb, 'i_q').reshape(b, t, IDX_HEADS, IDX_DIM).transpose(0, 2, 1, 3).astype(BF16)
        w_idx = r3(seg_s(rs_, 'i_w')) * idx_scale
        if grp == 0:
            o_c = jnp.concatenate([
                _dsa_prompt(big, row0 + bi * t, t, att_cols, q_idx[bi], w_idx[bi], k_idx[bi].astype(BF16),
                            rel_bias) for bi in range(b)], axis=0)
        else:
            cache_k, cache_v, cache_idx_k, page_table = sample_ctx
            o_c = _dsa_sample(big, row0, b, t, att_cols, q_idx.reshape(b, IDX_HEADS * t, IDX_DIM),
                              w_idx.transpose(0, 2, 1).reshape(b, IDX_HEADS * t, 1), k_idx,
                              cache_k.reshape(*cache_k.shape[:3], mw), cache_v.reshape(*cache_v.shape[:3], mw),
                              cache_idx_k, l, page_table, rel_bias)
        outs.append((o_a.reshape(b * t, mw).astype(BF16), o_b.reshape(b * t, mw).astype(BF16), o_c))
        states.append((k_att, v_att, k_idx, s_new, conv_new, c_new, n_new, m_new))

    oa = jnp.concatenate([outs[0][0], outs[1][0]], axis=0)
    ob = jnp.concatenate([outs[0][1], outs[1][1]], axis=0)
    oc = jnp.concatenate([outs[0][2], outs[1][2]], axis=0)
    x2 = _merge(x1, g1, oa, ob, oc, P['w_gates'][l], P['b_gates'][l][None], P['w_branch'][l], P['w_out'][l])
    x3 = _ffn(x2, P['norm_g'][l, 2][None], P['ffb_w13'][l], P['ffb_w2'][l])
    return x3, states


def _prep_ffn(w13, w2, tf=512):
    f = w2.shape[1]
    fp = _round_up(f, tf)
    w1 = jnp.pad(w13[:, :, :f], ((0, 0), (0, 0), (0, fp - f)))
    w3 = jnp.pad(w13[:, :, f:], ((0, 0), (0, 0), (0, fp - f)))
    w13p = jnp.concatenate([w1, w3], axis=-1).astype(BF16)
    w2p = jnp.pad(w2, ((0, 0), (0, fp - f), (0, 0))).astype(BF16)
    return w13p, w2p


def kernel(x_prompt, x_sample, cache_k, cache_v, cache_idx_k, state_gdn, state_gdn_conv, state_mlstm_c,
           state_mlstm_n, state_mlstm_m, page_table, norm_g, final_g, ffa_w13, ffa_w2, ffb_w13, ffb_w2,
           w_in, b_in, gdn_conv_w, gdn_a_log, gdn_dt_bias, gdn_norm_g, mlstm_norm_g, idx_ln_g, idx_ln_b,
           rel_bias, w_branch, w_out):
    bp, tp, d = x_prompt.shape
    bs, ts, _ = x_sample.shape
    depth = w_in.shape[0]
    seg = _segments(d)
    lay_big, _ = _layout(seg, _BIG)
    lay_small, n_small = _layout(seg, _SMALL)
    pad_small = _round_up(n_small, LANE) - n_small

    ffa_w13p, ffa_w2p = _prep_ffn(ffa_w13, ffa_w2)
    ffb_w13p, ffb_w2p = _prep_ffn(ffb_w13, ffb_w2)
    go, gw = seg['gates']
    P = dict(
        norm_g=norm_g,
        ffa_w13=ffa_w13p, ffa_w2=ffa_w2p,
        ffb_w13=ffb_w13p, ffb_w2=ffb_w2p,
        w_big=_take_cols(w_in, seg, _BIG).astype(BF16), b_big=_take_cols(b_in, seg, _BIG),
        w_small=jnp.pad(_take_cols(w_in, seg, _SMALL), ((0, 0), (0, 0), (0, pad_small))),
        b_small=jnp.pad(_take_cols(b_in, seg, _SMALL), ((0, 0), (0, pad_small))),
        w_gates=w_in[:, :, go:go + gw].astype(BF16), b_gates=b_in[:, go:go + gw],
        w_branch=w_branch.astype(BF16), w_out=w_out.astype(BF16),
        gdn_conv_w=gdn_conv_w, gdn_a_log=gdn_a_log, gdn_dt_bias=gdn_dt_bias, gdn_norm_g=gdn_norm_g,
        mlstm_norm_g=mlstm_norm_g, idx_ln_g=idx_ln_g, idx_ln_b=idx_ln_b,
        lay_big=lay_big, lay_small=lay_small,
    )

    n_prompt = bp * tp
    xs = jnp.concatenate([x_prompt.reshape(n_prompt, d), x_sample.reshape(bs * ts, d)], axis=0)
    init_sample = (state_gdn, state_gdn_conv, state_mlstm_c, state_mlstm_n, state_mlstm_m)
    sample_ctx = (cache_k, cache_v, cache_idx_k, page_table)
    p_st, s_st = [], []
    for l in range(depth):
        xs, (p, s) = _trunk_layer(xs, l, P, rel_bias, sample_ctx, n_prompt, (bp, tp), (bs, ts), init_sample)
        p_st.append(p)
        s_st.append(s)
    y = _final_norm(xs, final_g[None])
    y_prompt = y[:n_prompt].reshape(bp, tp, d)
    y_sample = y[n_prompt:].reshape(bs, ts, d)

    def stacked(states, i):
        return jnp.stack([s[i] for s in states], axis=0)

    return (y_prompt, y_sample,
            *[stacked(p_st, i) for i in range(8)],
            *[stacked(s_st, i) for i in range(8)])
```

```python
import functools
import math

import jax
import jax.numpy as jnp
from jax import lax
from jax.experimental import pallas as pl
from jax.experimental.pallas import tpu as pltpu

F32 = jnp.float32
BF16 = jnp.bfloat16

HEAD_DIM = 128
CONV_W = 4
CHUNK = 64
IDX_HEADS = 16
IDX_DIM = 64
TOPK_MAX = 256
Q_BLOCK = 128
NUM_BUCKETS = 32
MAX_DISTANCE = 128
PAGE_SIZE = 128
N_BRANCH = 3
EPS = 1e-6

LANE = 128
VMEM_LIMIT = 56 * 1024 * 1024


def _round_up(a, b):
    return (a + b - 1) // b * b


def _rms_rows(x, g):
    return x * lax.rsqrt(jnp.mean(x * x, axis=-1, keepdims=True) + EPS) * g


def _ffn_kernel(x_ref, g_ref, w1_ref, w3_ref, w2_ref, o_ref, n_ref):
    j = pl.program_id(1)

    @pl.when(j == 0)
    def _():
        n_ref[...] = _rms_rows(x_ref[...], g_ref[...]).astype(BF16)
        o_ref[...] = jnp.zeros_like(o_ref)

    n = n_ref[...]
    g = jnp.dot(n, w1_ref[...], preferred_element_type=F32)
    u = jnp.dot(n, w3_ref[...], preferred_element_type=F32)
    a = (g * jax.nn.sigmoid(g) * u).astype(BF16)
    o_ref[...] += jnp.dot(a, w2_ref[...], preferred_element_type=F32)

    @pl.when(j == pl.num_programs(1) - 1)
    def _():
        o_ref[...] = x_ref[...] + 0.5 * o_ref[...]


def _ffn(x, g, w13p, w2p, *, tm=512, tf=512):
    m, d = x.shape
    fp = w2p.shape[0]
    nj = fp // tf
    return pl.pallas_call(
        _ffn_kernel,
        grid=(m // tm, nj),
        in_specs=[
            pl.BlockSpec((tm, d), lambda i, j: (i, 0)),
            pl.BlockSpec((1, d), lambda i, j: (0, 0)),
            pl.BlockSpec((d, tf), lambda i, j: (0, j)),
            pl.BlockSpec((d, tf), lambda i, j: (0, j + nj)),
            pl.BlockSpec((tf, d), lambda i, j: (j, 0)),
        ],
        out_specs=pl.BlockSpec((tm, d), lambda i, j: (i, 0)),
        out_shape=jax.ShapeDtypeStruct((m, d), F32),
        scratch_shapes=[pltpu.VMEM((tm, d), BF16)],
        compiler_params=pltpu.CompilerParams(
            dimension_semantics=("parallel", "arbitrary"), vmem_limit_bytes=VMEM_LIMIT),
        name="ffn",
    )(x, g, w13p, w13p, w2p)


def _inproj_kernel(x_ref, g_ref, w_ref, b_ref, o_ref, n_ref):
    @pl.when(pl.program_id(1) == 0)
    def _():
        n_ref[...] = _rms_rows(x_ref[...], g_ref[...]).astype(BF16)

    o_ref[...] = jnp.dot(n_ref[...], w_ref[...], preferred_element_type=F32) + b_ref[...]


def _inproj(x, g, w, b, *, tm=512, tn=1024):
    m, d = x.shape
    n = w.shape[1]
    return pl.pallas_call(
        _inproj_kernel,
        grid=(m // tm, n // tn),
        in_specs=[
            pl.BlockSpec((tm, d), lambda i, j: (i, 0)),
            pl.BlockSpec((1, d), lambda i, j: (0, 0)),
            pl.BlockSpec((d, tn), lambda i, j: (0, j)),
            pl.BlockSpec((1, tn), lambda i, j: (0, j)),
        ],
        out_specs=pl.BlockSpec((tm, tn), lambda i, j: (i, j)),
        out_shape=jax.ShapeDtypeStruct((m, n), F32),
        scratch_shapes=[pltpu.VMEM((tm, d), BF16)],
        compiler_params=pltpu.CompilerParams(
            dimension_semantics=("parallel", "arbitrary"), vmem_limit_bytes=VMEM_LIMIT),
        name="inproj",
    )(x, g, w, b)


def _inproj_small_kernel(x_ref, g_ref, w_ref, b_ref, lg_ref, lb_ref, o_ref, ki_ref):
    n = _rms_rows(x_ref[...], g_ref[...])
    o = jnp.dot(n, w_ref[...], preferred_element_type=F32, precision=lax.Precision.HIGHEST) + b_ref[...]
    o_ref[...] = o
    ik = o[:, :IDX_DIM]
    mu = jnp.mean(ik, axis=-1, keepdims=True)
    var = jnp.mean(jnp.square(ik - mu), axis=-1, keepdims=True)
    ki_ref[...] = (ik - mu) * lax.rsqrt(var + EPS) * lg_ref[...] + lb_ref[...]


def _inproj_small(x, g, w, b, ln_g, ln_b, *, tm=512):
    m, d = x.shape
    n = w.shape[1]
    return pl.pallas_call(
        _inproj_small_kernel,
        grid=(m // tm,),
        in_specs=[
            pl.BlockSpec((tm, d), lambda i: (i, 0)),
            pl.BlockSpec((1, d), lambda i: (0, 0)),
            pl.BlockSpec((d, n), lambda i: (0, 0)),
            pl.BlockSpec((1, n), lambda i: (0, 0)),
            pl.BlockSpec((1, IDX_DIM), lambda i: (0, 0)),
            pl.BlockSpec((1, IDX_DIM), lambda i: (0, 0)),
        ],
        out_specs=[pl.BlockSpec((tm, n), lambda i: (i, 0)), pl.BlockSpec((tm, IDX_DIM), lambda i: (i, 0))],
        out_shape=[jax.ShapeDtypeStruct((m, n), F32), jax.ShapeDtypeStruct((m, IDX_DIM), F32)],
        compiler_params=pltpu.CompilerParams(
            dimension_semantics=("parallel",), vmem_limit_bytes=VMEM_LIMIT),
        name="inproj_small",
    )(x, g, w, b, ln_g, ln_b)


def _merge_kernel(x_ref, g_ref, oa_ref, ob_ref, oc_ref, wg0_ref, wg1_ref, wg2_ref,
                  bg0_ref, bg1_ref, bg2_ref, wb0_ref, wb1_ref, wb2_ref, wo_ref, o_ref, n_ref):
    j = pl.program_id(1)

    @pl.when(j == 0)
    def _():
        n_ref[...] = _rms_rows(x_ref[...], g_ref[...]).astype(BF16)
        o_ref[...] = jnp.zeros_like(o_ref)

    n = n_ref[...]
    merged = None
    for o_r, wg_r, bg_r, wb_r in ((oa_ref, wg0_ref, bg0_ref, wb0_ref),
                                  (ob_ref, wg1_ref, bg1_ref, wb1_ref),
                                  (oc_ref, wg2_ref, bg2_ref, wb2_ref)):
        gate = jax.nn.sigmoid(jnp.dot(n, wg_r[...], preferred_element_type=F32) + bg_r[...])
        y = jnp.dot(o_r[...], wb_r[0], preferred_element_type=F32)
        merged = gate * y if merged is None else merged + gate * y
    o_ref[...] += jnp.dot(merged.astype(BF16), wo_ref[...], preferred_element_type=F32)

    @pl.when(j == pl.num_programs(1) - 1)
    def _():
        o_ref[...] = x_ref[...] + o_ref[...]


def _merge(x, g, oa, ob, oc, wg, bg, wb, wo, *, tm=512, tn=512):
    m, d = x.shape
    w = oa.shape[1]
    nj = d // tn
    wg_specs = [pl.BlockSpec((d, tn), functools.partial(lambda i, j, n: (0, n * nj + j), n=n))
                for n in range(N_BRANCH)]
    bg_specs = [pl.BlockSpec((1, tn), functools.partial(lambda i, j, n: (0, n * nj + j), n=n))
                for n in range(N_BRANCH)]
    wb_specs = [pl.BlockSpec((1, w, tn), functools.partial(lambda i, j, n: (n, 0, j), n=n))
                for n in range(N_BRANCH)]
    o_spec = pl.BlockSpec((tm, w), lambda i, j: (i, 0))
    return pl.pallas_call(
        _merge_kernel,
        grid=(m // tm, nj),
        in_specs=[pl.BlockSpec((tm, d), lambda i, j: (i, 0)),
                  pl.BlockSpec((1, d), lambda i, j: (0, 0)),
                  o_spec, o_spec, o_spec,
                  *wg_specs, *bg_specs, *wb_specs,
                  pl.BlockSpec((tn, d), lambda i, j: (j, 0))],
        out_specs=pl.BlockSpec((tm, d), lambda i, j: (i, 0)),
        out_shape=jax.ShapeDtypeStruct((m, d), F32),
        scratch_shapes=[pltpu.VMEM((tm, d), BF16)],
        compiler_params=pltpu.CompilerParams(
            dimension_semantics=("parallel", "arbitrary"), vmem_limit_bytes=VMEM_LIMIT),
        name="merge",
    )(x, g, oa, ob, oc, wg, wg, wg, bg, bg, bg, wb, wb, wb, wo)


def _norm_kernel(x_ref, g_ref, o_ref):
    o_ref[...] = _rms_rows(x_ref[...], g_ref[...])


def _final_norm(x, g, *, tm=512):
    m, d = x.shape
    return pl.pallas_call(
        _norm_kernel,
        grid=(m // tm,),
        in_specs=[pl.BlockSpec((tm, d), lambda i: (i, 0)), pl.BlockSpec((1, d), lambda i: (0, 0))],
        out_specs=pl.BlockSpec((tm, d), lambda i: (i, 0)),
        out_shape=jax.ShapeDtypeStruct((m, d), F32),
        compiler_params=pltpu.CompilerParams(dimension_semantics=("parallel",)),
        name="final_norm",
    )(x, g)


NEG = -1e30
INT_MIN = -2 ** 31


def _order_key(x):
    bits = lax.bitcast_convert_type(x, jnp.int32)
    return bits ^ ((bits >> 31) & jnp.int32(0x7FFFFFFF))


def _kth_largest_key(count_ge, topk, shape):
    t0 = jnp.where(count_ge(jnp.zeros(shape, jnp.int32)) >= topk, jnp.int32(0), jnp.int32(INT_MIN))

    def bit(b, t):
        cand = t + lax.shift_left(jnp.int32(1), jnp.int32(30) - b)
        return jnp.where(count_ge(cand) >= topk, cand, t)

    return lax.fori_loop(0, 31, bit, t0)


def _kth_largest_key_radix(count_ge, topk, shape, bits):
    t = jnp.where(count_ge(jnp.zeros(shape, jnp.int32)) >= topk, jnp.int32(0), jnp.int32(INT_MIN))
    hi = 31
    while hi > 0:
        nb = hi % bits or bits
        sh = hi - nb
        digit = jnp.zeros(shape, jnp.int32)
        for d in range(1, 2 ** nb):
            digit = digit + jnp.where(count_ge(t + jnp.int32(d << sh)) >= topk, 1, 0)
        t = t + lax.shift_left(digit, jnp.int32(sh))
        hi = sh
    return t


def _softmax_update(s, vb, m_prev, l_prev, acc_prev):
    reps = s.shape[1] // LANE
    m_new = jnp.maximum(m_prev, jnp.max(s, axis=1, keepdims=True))
    alpha = jnp.exp(m_prev - m_new)
    p = jnp.exp(s - jnp.concatenate([m_new] * reps, axis=1))
    l_new = alpha * l_prev + jnp.sum(p, axis=1, keepdims=True)
    pv = jnp.dot(p.astype(BF16), vb, preferred_element_type=F32)
    dreps = pv.shape[1] // LANE
    acc_new = jnp.concatenate([alpha] * dreps, axis=1) * acc_prev + pv
    return m_new, l_new, acc_new


def _dsa_prompt_kernel(far_ref, qi_ref, w_ref, ki_ref, q_ref, k_ref, v_ref, bd_ref, bs_ref, o_ref,
                       key_ref, thr_ref, wb_ref, qb_ref, m_ref, l_ref, acc_ref, *, topk, tq, nh):
    i = pl.program_id(0)
    j = pl.program_id(1)
    rg = 128

    @pl.when(j == 0)
    def _select():
        for h in range(IDX_HEADS):
            wb_ref[h] = jnp.broadcast_to(w_ref[:, h:h + 1], (tq, tq))
        qb_ref[...] = (q_ref[...] * HEAD_DIM ** -0.5).astype(BF16)
        m_ref[...] = jnp.full_like(m_ref, NEG)
        l_ref[...] = jnp.zeros_like(l_ref)
        acc_ref[...] = jnp.zeros_like(acc_ref)
        qi = qi_ref[...].reshape(IDX_HEADS * tq, IDX_DIM)
        row = lax.broadcasted_iota(jnp.int32, (tq, tq), 0)
        col = lax.broadcasted_iota(jnp.int32, (tq, tq), 1)

        def chunk(c, carry):
            kc = ki_ref[pl.ds(pl.multiple_of(c * tq, tq), tq), :]
            s = lax.dot_general(qi, kc, (((1,), (1,)), ((), ())), preferred_element_type=F32)
            sc = jnp.zeros((tq, tq), F32)
            for h in range(IDX_HEADS):
                sc = sc + jnp.maximum(s[h * tq:(h + 1) * tq], 0.0) * wb_ref[h]
            sc = jnp.where(col <= row + jnp.where(c < i, tq, 0), sc, -jnp.inf)
            key_ref[c] = _order_key(sc)
            return carry

        lax.fori_loop(0, i + 1, chunk, 0)

        for r in range(tq // rg):
            rows = pl.ds(r * rg, rg)

            def count_ge(cand):
                def body(c, a):
                    hit = jnp.where(key_ref[c, rows, :] >= cand, 1.0, 0.0)
                    for x in range(tq // LANE):
                        a = a + hit[:, x * LANE:(x + 1) * LANE]
                    return a
                a = lax.fori_loop(0, i + 1, body, jnp.zeros((rg, LANE), F32))
                return jnp.sum(a, axis=1, keepdims=True)

            t = _kth_largest_key(count_ge, float(topk), (rg, 1))
            thr_ref[rows, :] = jnp.broadcast_to(t, (rg, LANE))

    def attend(bias_of_head):
        kk = key_ref[j]
        thr = thr_ref[...]
        selb = jnp.concatenate(
            [jnp.where(kk[:, x * LANE:(x + 1) * LANE] >= thr, 0.0, NEG) for x in range(tq // LANE)], axis=1)
        kb = k_ref[...].astype(BF16)
        vb = v_ref[...].astype(BF16)
        for h in range(nh):
            hs = slice(h * HEAD_DIM, (h + 1) * HEAD_DIM)
            s = lax.dot_general(qb_ref[:, hs], kb[:, hs], (((1,), (1,)), ((), ())),
                                preferred_element_type=F32)
            s = s + selb + bias_of_head(h)
            m_new, l_new, acc_new = _softmax_update(s, vb[:, hs], m_ref[h], l_ref[h], acc_ref[:, hs])
            m_ref[h] = m_new
            l_ref[h] = l_new
            acc_ref[:, hs] = acc_new

    @pl.when(j < i - 1)
    def _far():
        attend(lambda h: far_ref[h])

    @pl.when(j == i - 1)
    def _sub():
        attend(lambda h: bs_ref[h])

    @pl.when(j == i)
    def _diag():
        attend(lambda h: bd_ref[h])
        for h in range(nh):
            hs = slice(h * HEAD_DIM, (h + 1) * HEAD_DIM)
            o_ref[:, hs] = (acc_ref[:, hs] / l_ref[h]).astype(BF16)


def _t5_bucket(dist):
    n = jnp.maximum(dist, 0)
    max_exact = NUM_BUCKETS // 2
    nf = jnp.maximum(n, 1).astype(F32)
    large = max_exact + (jnp.log(nf / max_exact) / math.log(MAX_DISTANCE / max_exact)
                         * (NUM_BUCKETS - max_exact)).astype(jnp.int32)
    large = jnp.minimum(large, NUM_BUCKETS - 1)
    return jnp.where(n < max_exact, n, large)


def _dsa_prompt(big, row0, t, cols, qi, w, ki, rel_bias, *, tq=256):
    mw = rel_bias.shape[1] * HEAD_DIM
    nh = rel_bias.shape[1]
    assert t % tq == 0 and row0 % tq == 0 and tq >= MAX_DISTANCE
    nq = t // tq
    rb = row0 // tq
    topk = min(TOPK_MAX, t // 4)
    r = jnp.arange(tq)[:, None]
    c = jnp.arange(tq)[None, :]
    bd = jnp.where((r >= c)[None], jnp.moveaxis(rel_bias[_t5_bucket(r - c)], -1, 0), NEG)
    bs = jnp.moveaxis(rel_bias[_t5_bucket(tq + r - c)], -1, 0)
    far = rel_bias[_t5_bucket(jnp.int32(tq + 1))]
    cq, ck, cv = cols
    kern = functools.partial(_dsa_prompt_kernel, topk=topk, tq=tq, nh=nh)
    return pl.pallas_call(
        kern,
        grid=(nq, nq),
        in_specs=[
            pl.BlockSpec(memory_space=pltpu.SMEM),
            pl.BlockSpec((IDX_HEADS, tq, IDX_DIM), lambda i, j: (0, i, 0)),
            pl.BlockSpec((tq, IDX_HEADS), lambda i, j: (i, 0)),
            pl.BlockSpec((t, IDX_DIM), lambda i, j: (0, 0)),
            pl.BlockSpec((tq, mw), lambda i, j: (rb + i, cq)),
            pl.BlockSpec((tq, mw), lambda i, j: (rb + jnp.minimum(j, i), ck)),
            pl.BlockSpec((tq, mw), lambda i, j: (rb + jnp.minimum(j, i), cv)),
            pl.BlockSpec((nh, tq, tq), lambda i, j: (0, 0, 0)),
            pl.BlockSpec((nh, tq, tq), lambda i, j: (0, 0, 0)),
        ],
        out_specs=pl.BlockSpec((tq, mw), lambda i, j: (i, 0)),
        out_shape=jax.ShapeDtypeStruct((t, mw), BF16),
        scratch_shapes=[
            pltpu.VMEM((nq, tq, tq), jnp.int32),
            pltpu.VMEM((tq, LANE), jnp.int32),
            pltpu.VMEM((IDX_HEADS, tq, tq), F32),
            pltpu.VMEM((tq, mw), BF16),
            pltpu.VMEM((nh, tq, LANE), F32),
            pltpu.VMEM((nh, tq, LANE), F32),
            pltpu.VMEM((tq, mw), F32),
        ],
        compiler_params=pltpu.CompilerParams(
            dimension_semantics=("parallel", "arbitrary"), vmem_limit_bytes=VMEM_LIMIT),
        name="dsa_prompt",
    )(far, qi, w, ki, big, big, big, bd, bs)


def _dsa_sample_kernel(pt_ref, qi_ref, w_ref, kin_ref, q_ref, kn_ref, vn_ref, bias_ref, biasn_ref, *rest,
                       topk, t, n_pages, g_pages, nh, past):
    idx_refs = rest[:n_pages]
    kp_refs = rest[n_pages:n_pages + g_pages]
    vp_refs = rest[n_pages + g_pages:n_pages + 2 * g_pages]
    o_ref, selb_ref, selbn_ref, qbd_ref, m_ref, l_ref, acc_ref = rest[n_pages + 2 * g_pages:]
    g = pl.program_id(1)
    ng = n_pages // g_pages
    gw = g_pages * PAGE_SIZE
    rows = nh * t
    mw = nh * HEAD_DIM

    @pl.when(g == 0)
    def _select():
        qi = qi_ref[...]
        wcol = w_ref[...]
        pad = jnp.zeros((PAGE_SIZE - t, IDX_DIM), F32)
        keys = jnp.concatenate([r[...] for r in idx_refs] + [kin_ref[...], pad], axis=0).astype(BF16)
        s = lax.dot_general(qi, keys, (((1,), (1,)), ((), ())), preferred_element_type=F32)
        s = jnp.maximum(s, 0.0) * wcol
        sc = s[0:t]
        for h in range(1, IDX_HEADS):
            sc = sc + s[h * t:(h + 1) * t]
        kpos = lax.broadcasted_iota(jnp.int32, sc.shape, 1)
        qrow = lax.broadcasted_iota(jnp.int32, sc.shape, 0)
        key = _order_key(jnp.where(kpos <= past + qrow, sc, -jnp.inf))

        def count_ge(cand):
            return jnp.sum(jnp.where(key >= cand, 1.0, 0.0), axis=1, keepdims=True)

        thr = _kth_largest_key_radix(count_ge, float(topk), (t, 1), 4)
        selb = jnp.where(key >= thr, 0.0, NEG)
        for x in range(ng):
            selb_ref[x] = selb[:, x * gw:(x + 1) * gw]
        selbn_ref[...] = selb[:, past:]

        qs = jnp.concatenate([q_ref[...] * HEAD_DIM ** -0.5] * nh, axis=0)
        rr = lax.broadcasted_iota(jnp.int32, (rows, mw), 0) >> (t.bit_length() - 1)
        cc = lax.broadcasted_iota(jnp.int32, (rows, mw), 1) >> (HEAD_DIM.bit_length() - 1)
        qbd_ref[...] = jnp.where(rr == cc, qs, 0.0).astype(BF16)
        m_ref[...] = jnp.full_like(m_ref, NEG)
        l_ref[...] = jnp.zeros_like(l_ref)
        acc_ref[...] = jnp.zeros_like(acc_ref)

    def update(kb, vb, bias):
        s = lax.dot_general(qbd_ref[...], kb, (((1,), (1,)), ((), ())), preferred_element_type=F32)
        m_new, l_new, acc_new = _softmax_update(s + bias, vb, m_ref[...], l_ref[...], acc_ref[...])
        m_ref[...] = m_new
        l_ref[...] = l_new
        acc_ref[...] = acc_new

    def page(r):
        return jnp.concatenate([r[pl.ds(h, PAGE_SIZE, stride=nh), :] for h in range(nh)], axis=1)

    kb = jnp.concatenate([page(r) for r in kp_refs], axis=0).astype(BF16)
    vb = jnp.concatenate([page(r) for r in vp_refs], axis=0).astype(BF16)
    update(kb, vb, bias_ref[g] + jnp.concatenate([selb_ref[g]] * nh, axis=0))

    @pl.when(g == ng - 1)
    def _finish():
        padn = jnp.zeros((PAGE_SIZE - t, mw), F32)
        kn = jnp.concatenate([kn_ref[...], padn], axis=0).astype(BF16)
        vn = jnp.concatenate([vn_ref[...], padn], axis=0).astype(BF16)
        update(kn, vn, biasn_ref[...] + jnp.concatenate([selbn_ref[...]] * nh, axis=0))
        for h in range(nh):
            hs = slice(h * HEAD_DIM, (h + 1) * HEAD_DIM)
            rs = slice(h * t, (h + 1) * t)
            o_ref[:, hs] = (acc_ref[rs, hs] / l_ref[rs, :]).astype(BF16)


def _dsa_sample(big, row0, b, t, cols, qi, w, kin, cache_k, cache_v, cache_idx_k, l, page_table, rel_bias):
    nh = rel_bias.shape[1]
    mw = nh * HEAD_DIM
    n_pages = page_table.shape[1]
    past = n_pages * PAGE_SIZE
    g_pages = math.gcd(n_pages, 4)
    ng = n_pages // g_pages
    gw = g_pages * PAGE_SIZE
    topk = min(TOPK_MAX, (past + t) // 4)
    assert row0 % t == 0 and t % 8 == 0 and t <= PAGE_SIZE and t & (t - 1) == 0
    rb = row0 // t
    rows = nh * t
    qpos = past + jnp.arange(t)
    kpos = jnp.arange(past + PAGE_SIZE)
    dist = qpos[:, None] - kpos[None, :]
    bias = jnp.where((dist >= 0)[None], jnp.moveaxis(rel_bias[_t5_bucket(dist)], -1, 0), NEG)
    bias = bias.reshape(rows, past + PAGE_SIZE)
    bias_past = bias[:, :past].reshape(rows, ng, gw).transpose(1, 0, 2)
    bias_new = bias[:, past:]
    cq, ck, cv = cols

    def page_map(x, bi, gi, pt):
        return (l, pt[bi, gi * g_pages + x], 0, 0)

    def idx_map(x, bi, gi, pt):
        return (l, pt[bi, x], 0, 0)

    in_specs = [
        pl.BlockSpec((None, IDX_HEADS * t, IDX_DIM), lambda bi, gi, pt: (bi, 0, 0)),
        pl.BlockSpec((None, IDX_HEADS * t, 1), lambda bi, gi, pt: (bi, 0, 0)),
        pl.BlockSpec((None, t, IDX_DIM), lambda bi, gi, pt: (bi, 0, 0)),
        pl.BlockSpec((t, mw), lambda bi, gi, pt: (rb + bi, cq)),
        pl.BlockSpec((t, mw), lambda bi, gi, pt: (rb + bi, ck)),
        pl.BlockSpec((t, mw), lambda bi, gi, pt: (rb + bi, cv)),
        pl.BlockSpec((ng, rows, gw), lambda bi, gi, pt: (0, 0, 0)),
        pl.BlockSpec((rows, PAGE_SIZE), lambda bi, gi, pt: (0, 0)),
    ]
    in_specs += [pl.BlockSpec((None, None, PAGE_SIZE, IDX_DIM), functools.partial(idx_map, x))
                 for x in range(n_pages)]
    in_specs += [pl.BlockSpec((None, None, PAGE_SIZE * nh, HEAD_DIM), functools.partial(page_map, x % g_pages))
                 for x in range(2 * g_pages)]
    kern = functools.partial(_dsa_sample_kernel, topk=topk, t=t, n_pages=n_pages, g_pages=g_pages,
                             nh=nh, past=past)
    grid_spec = pltpu.PrefetchScalarGridSpec(
        num_scalar_prefetch=1,
        grid=(b, ng),
        in_specs=in_specs,
        out_specs=pl.BlockSpec((t, mw), lambda bi, gi, pt: (bi, 0)),
        scratch_shapes=[
            pltpu.VMEM((ng, t, gw), F32),
            pltpu.VMEM((t, PAGE_SIZE), F32),
            pltpu.VMEM((rows, mw), BF16),
            pltpu.VMEM((rows, LANE), F32),
            pltpu.VMEM((rows, LANE), F32),
            pltpu.VMEM((rows, mw), F32),
        ],
    )
    return pl.pallas_call(
        kern,
        grid_spec=grid_spec,
        out_shape=jax.ShapeDtypeStruct((b * t, mw), BF16),
        compiler_params=pltpu.CompilerParams(
            dimension_semantics=("parallel", "arbitrary"), vmem_limit_bytes=VMEM_LIMIT),
        name="dsa_sample",
    )(page_table, qi, w, kin, big, big, big, bias_past, bias_new,
      *([cache_idx_k] * n_pages), *([cache_k] * g_pages), *([cache_v] * g_pages))


def _lane_expand(x, nh):
    c = x.shape[0]
    return jnp.concatenate([jnp.broadcast_to(x[:, h:h + 1], (c, HEAD_DIM)) for h in range(nh)], axis=1)


def _bdot(a, b):
    return jnp.dot(a.astype(BF16), b.astype(BF16), preferred_element_type=F32)


def _bdot_nt(a, b):
    return lax.dot_general(a.astype(BF16), b.astype(BF16), (((1,), (1,)), ((), ())), preferred_element_type=F32)


def _bdot_tn(a, b):
    return lax.dot_general(a.astype(BF16), b.astype(BF16), (((0,), (0,)), ((), ())), preferred_element_type=F32)


def _softplus(x):
    return jnp.maximum(x, 0.0) + jnp.log(1.0 + jnp.exp(-jnp.abs(x)))


def _gdn_kernel(qkv_ref, z_ref, sm_ref, cw_ref, alog_ref, dtb_ref, ng_ref, s0_ref, conv0_ref,
                o_ref, s_out_ref, conv_out_ref, s_ref, xp_ref, *, c, nh, ob, oa):
    n = pl.program_id(1)
    mw = nh * HEAD_DIM
    halo = 8

    @pl.when(n == 0)
    def _():
        s_ref[...] = s0_ref[...]
        xp_ref[0:halo, :] = conv0_ref[...]

    xp_ref[halo:halo + c, :] = qkv_ref[...]
    y = cw_ref[0:1, :] * xp_ref[halo - CONV_W + 1:halo - CONV_W + 1 + c, :]
    for j in range(1, CONV_W):
        y = y + cw_ref[j:j + 1, :] * xp_ref[halo - CONV_W + 1 + j:halo - CONV_W + 1 + j + c, :]
    act = y * jax.nn.sigmoid(y)

    sm = sm_ref[...]
    beta = jax.nn.sigmoid(sm[:, ob:ob + nh])
    g = -jnp.exp(alog_ref[...]) * _softplus(sm[:, oa:oa + nh] + dtb_ref[...])
    beta_e = _lane_expand(beta, nh)
    g_e = _lane_expand(g, nh)
    t_i = lax.broadcasted_iota(jnp.int32, (c, mw), 0)
    j_i = lax.broadcasted_iota(jnp.int32, (c, mw), 1) & (HEAD_DIM - 1)
    ii = lax.broadcasted_iota(jnp.int32, (c, c), 0)
    jj = lax.broadcasted_iota(jnp.int32, (c, c), 1)
    tri = jnp.where(jj <= ii, 1.0, 0.0)
    cum = jnp.dot(tri, jnp.concatenate([g_e, jnp.where(t_i > j_i, g_e, 0.0)], axis=1),
                  preferred_element_type=F32, precision=lax.Precision.HIGHEST)
    gc = cum[:, :mw]
    dmat = cum[:, mw:]
    g_end = gc[c - 1:c, :]
    e_gc = jnp.exp(gc)
    e_rest = jnp.exp(g_end - gc)
    e_end = jnp.exp(g_end)
    eye = jnp.where(ii == jj, 1.0, 0.0)

    heads = range(nh)
    hsl = [slice(h * HEAD_DIM, (h + 1) * HEAD_DIM) for h in heads]
    qn, kn, kb, dec = [], [], [], []
    for h in heads:
        qh = act[:, h * HEAD_DIM:(h + 1) * HEAD_DIM]
        kh = act[:, mw + h * HEAD_DIM:mw + (h + 1) * HEAD_DIM]
        qn.append(qh * lax.rsqrt(jnp.sum(qh * qh, axis=1, keepdims=True) + EPS) * HEAD_DIM ** -0.5)
        kn.append(kh * lax.rsqrt(jnp.sum(kh * kh, axis=1, keepdims=True) + EPS))
        kb.append(kn[h] * beta_e[:, hsl[h]])
        dec.append(jnp.where(ii >= jj, jnp.exp(dmat[:, h * HEAD_DIM:h * HEAD_DIM + c]), 0.0))
    knb = [kn[h].astype(BF16) for h in heads]
    kk = [_bdot_nt(kb[h], knb[h]) for h in heads]
    qk = [_bdot_nt(qn[h], knb[h]) for h in heads]
    x = [jnp.where(ii > jj, -kk[h] * dec[h], 0.0) for h in heads]
    p = [eye + x[h] for h in heads]
    for _ in range(c.bit_length() - 2):
        x = [_bdot(x[h], x[h]) for h in heads]
        p = [p[h] + _bdot(p[h], x[h]) for h in heads]
    uw = [_bdot(p[h], jnp.concatenate(
        [act[:, 2 * mw + h * HEAD_DIM:2 * mw + (h + 1) * HEAD_DIM] * beta_e[:, hsl[h]],
         kb[h] * e_gc[:, hsl[h]]], axis=1)) for h in heads]
    s_old = [s_ref[h] for h in heads]
    sb = [s_old[h].astype(BF16) for h in heads]
    o_inter = [_bdot(qn[h] * e_gc[:, hsl[h]], sb[h]) for h in heads]
    v_new = [uw[h][:, :HEAD_DIM] - _bdot(uw[h][:, HEAD_DIM:], sb[h]) for h in heads]
    o_intra = [_bdot(qk[h] * dec[h], v_new[h]) for h in heads]
    kv = [_bdot_tn(kn[h] * e_rest[:, hsl[h]], v_new[h]) for h in heads]
    for h in heads:
        s_ref[h] = s_old[h] * e_end[:, hsl[h]] + kv[h]
        o = o_inter[h] + o_intra[h]
        on = o * lax.rsqrt(jnp.mean(o * o, axis=1, keepdims=True) + EPS) * ng_ref[...]
        zh = z_ref[:, hsl[h]]
        o_ref[:, hsl[h]] = (on * zh * jax.nn.sigmoid(zh)).astype(o_ref.dtype)

    tail = xp_ref[c:c + halo, :]
    xp_ref[0:halo, :] = tail

    @pl.when(n == pl.num_programs(1) - 1)
    def _():
        s_out_ref[...] = s_ref[...]
        conv_out_ref[...] = tail


def _gdn(big, small, row0, b, t, cq, cz, ob, oa, conv_w, a_log, dt_bias, norm_g, s0, conv0):
    nh = s0.shape[1]
    mw = nh * HEAD_DIM
    c = math.gcd(t, CHUNK)
    assert c % 8 == 0 and c & (c - 1) == 0 and row0 % c == 0
    nc = t // c
    rb = row0 // c
    halo = 8
    conv0p = jnp.pad(conv0, ((0, 0), (halo - (CONV_W - 1), 0), (0, 0)))
    odt = BF16 if c % 16 == 0 else F32
    kern = functools.partial(_gdn_kernel, c=c, nh=nh, ob=ob, oa=oa)
    o, s_new, conv_new = pl.pallas_call(
        kern,
        grid=(b, nc),
        in_specs=[
            pl.BlockSpec((c, 3 * mw), lambda bi, n: (rb + bi * nc + n, cq)),
            pl.BlockSpec((c, mw), lambda bi, n: (rb + bi * nc + n, cz)),
            pl.BlockSpec((c, LANE), lambda bi, n: (rb + bi * nc + n, 0)),
            pl.BlockSpec((CONV_W, 3 * mw), lambda bi, n: (0, 0)),
            pl.BlockSpec((1, nh), lambda bi, n: (0, 0)),
            pl.BlockSpec((1, nh), lambda bi, n: (0, 0)),
            pl.BlockSpec((1, HEAD_DIM), lambda bi, n: (0, 0)),
            pl.BlockSpec((None, nh, HEAD_DIM, HEAD_DIM), lambda bi, n: (bi, 0, 0, 0)),
            pl.BlockSpec((None, halo, 3 * mw), lambda bi, n: (bi, 0, 0)),
        ],
        out_specs=[
            pl.BlockSpec((c, mw), lambda bi, n: (bi * nc + n, 0)),
            pl.BlockSpec((None, nh, HEAD_DIM, HEAD_DIM), lambda bi, n: (bi, 0, 0, 0)),
            pl.BlockSpec((None, halo, 3 * mw), lambda bi, n: (bi, 0, 0)),
        ],
        out_shape=[
            jax.ShapeDtypeStruct((b * t, mw), odt),
            jax.ShapeDtypeStruct((b, nh, HEAD_DIM, HEAD_DIM), F32),
            jax.ShapeDtypeStruct((b, halo, 3 * mw), F32),
        ],
        scratch_shapes=[pltpu.VMEM((nh, HEAD_DIM, HEAD_DIM), F32), pltpu.VMEM((halo + c, 3 * mw), F32)],
        compiler_params=pltpu.CompilerParams(
            dimension_semantics=("parallel", "arbitrary"), vmem_limit_bytes=VMEM_LIMIT),
        name="gdn",
    )(big, big, small, conv_w, a_log[None], dt_bias[None], norm_g[None], s0, conv0p)
    return o, s_new, conv_new[:, halo - (CONV_W - 1):]


def _mlstm_kernel(q_ref, k_ref, v_ref, og_ref, sm_ref, ng_ref, c0_ref, n0_ref, m0_ref,
                  o_ref, c_out_ref, n_out_ref, m_out_ref, c_ref, n_ref, m_ref, *, c, nh, oi, of):
    n = pl.program_id(1)
    mw = nh * HEAD_DIM

    @pl.when(n == 0)
    def _():
        c_ref[...] = c0_ref[...]
        n_ref[...] = n0_ref[...]
        m_ref[...] = m0_ref[...]

    sm = sm_ref[...]
    ig_e = _lane_expand(sm[:, oi:oi + nh], nh)
    lf_e = _lane_expand(-_softplus(-sm[:, of:of + nh]), nh)
    t_i = lax.broadcasted_iota(jnp.int32, (c, mw), 0)
    j_i = lax.broadcasted_iota(jnp.int32, (c, mw), 1) & (HEAD_DIM - 1)
    ii = lax.broadcasted_iota(jnp.int32, (c, c), 0)
    jj = lax.broadcasted_iota(jnp.int32, (c, c), 1)
    lhs = jnp.concatenate([jnp.where(jj <= ii, 1.0, 0.0), jnp.ones((c, c), F32)], axis=1)
    rhs = jnp.concatenate([
        jnp.concatenate([lf_e, jnp.where(t_i > j_i, lf_e, 0.0)], axis=1),
        jnp.concatenate([jnp.zeros((c, mw), F32), jnp.where(t_i == j_i, ig_e, 0.0)], axis=1)], axis=0)
    cum = jnp.dot(lhs, rhs, preferred_element_type=F32, precision=lax.Precision.HIGHEST)
    bc = cum[:, :mw]
    dlog = cum[:, mw:]
    b_end = bc[c - 1:c, :]
    a_end = b_end - bc + ig_e

    heads = range(nh)
    hsl = [slice(h * HEAD_DIM, (h + 1) * HEAD_DIM) for h in heads]
    qb = [(q_ref[:, hsl[h]] * HEAD_DIM ** -0.5).astype(BF16) for h in heads]
    kh = [k_ref[:, hsl[h]] for h in heads]
    vb = [v_ref[:, hsl[h]].astype(BF16) for h in heads]
    cs = [c_ref[h] for h in heads]
    qk = [_bdot_nt(qb[h], kh[h]) for h in heads]
    qc = [_bdot(qb[h], cs[h]) for h in heads]
    m_prev = [m_ref[h:h + 1, :] for h in heads]
    m_new = [jnp.maximum(b_end[:, hsl[h]] + m_prev[h], jnp.max(a_end[:, hsl[h]], axis=0, keepdims=True))
             for h in heads]
    kw = [kh[h] * jnp.exp(a_end[:, hsl[h]] - m_new[h]) for h in heads]
    kv = [_bdot_tn(kw[h], vb[h]) for h in heads]
    mt, sc = [], []
    for h in heads:
        dl = jnp.where(ii >= jj, dlog[:, h * HEAD_DIM:h * HEAD_DIM + c], -jnp.inf)
        inter = bc[:, hsl[h]] + m_prev[h]
        mt.append(jnp.maximum(inter, jnp.max(dl, axis=1, keepdims=True)))
        sc.append(qk[h] * jnp.exp(dl - mt[h][:, :c]))
    sv = [_bdot(sc[h], vb[h]) for h in heads]
    for h in heads:
        w_inter = jnp.exp(bc[:, hsl[h]] + m_prev[h] - mt[h])
        ns = n_ref[h:h + 1, :]
        qf = q_ref[:, hsl[h]] * HEAD_DIM ** -0.5
        num = w_inter * qc[h] + sv[h]
        den = w_inter * jnp.sum(qf * ns, axis=1, keepdims=True) + jnp.sum(sc[h], axis=1, keepdims=True)
        hh = num / jnp.maximum(jnp.abs(den), jnp.exp(-mt[h]))
        keep = jnp.exp(b_end[:, hsl[h]] + m_prev[h] - m_new[h])
        c_ref[h] = keep * cs[h] + kv[h]
        n_ref[h:h + 1, :] = keep * ns + jnp.sum(kw[h], axis=0, keepdims=True)
        m_ref[h:h + 1, :] = m_new[h]
        hn = hh * lax.rsqrt(jnp.mean(hh * hh, axis=1, keepdims=True) + EPS) * ng_ref[...]
        o_ref[:, hsl[h]] = (hn * jax.nn.sigmoid(og_ref[:, hsl[h]])).astype(o_ref.dtype)

    @pl.when(n == pl.num_programs(1) - 1)
    def _():
        c_out_ref[...] = c_ref[...]
        n_out_ref[...] = n_ref[...]
        m_out_ref[...] = m_ref[...]


def _mlstm(big, small, row0, b, t, cols, oi, of, norm_g, c0, n0, m0):
    nh = c0.shape[1]
    mw = nh * HEAD_DIM
    c = math.gcd(t, CHUNK)
    assert c % 8 == 0 and c <= HEAD_DIM and row0 % c == 0
    nc = t // c
    rb = row0 // c
    odt = BF16 if c % 16 == 0 else F32
    m0e = jnp.broadcast_to(m0[:, :, None], (b, nh, LANE))

    def col_spec(cc):
        return pl.BlockSpec((c, mw), lambda bi, n: (rb + bi * nc + n, cc))

    kern = functools.partial(_mlstm_kernel, c=c, nh=nh, oi=oi, of=of)
    o, c_new, n_new, m_new = pl.pallas_call(
        kern,
        grid=(b, nc),
        in_specs=[
            *[col_spec(cc) for cc in cols],
            pl.BlockSpec((c, LANE), lambda bi, n: (rb + bi * nc + n, 0)),
            pl.BlockSpec((1, HEAD_DIM), lambda bi, n: (0, 0)),
            pl.BlockSpec((None, nh, HEAD_DIM, HEAD_DIM), lambda bi, n: (bi, 0, 0, 0)),
            pl.BlockSpec((None, nh, HEAD_DIM), lambda bi, n: (bi, 0, 0)),
            pl.BlockSpec((None, nh, LANE), lambda bi, n: (bi, 0, 0)),
        ],
        out_specs=[
            pl.BlockSpec((c, mw), lambda bi, n: (bi * nc + n, 0)),
            pl.BlockSpec((None, nh, HEAD_DIM, HEAD_DIM), lambda bi, n: (bi, 0, 0, 0)),
            pl.BlockSpec((None, nh, HEAD_DIM), lambda bi, n: (bi, 0, 0)),
            pl.BlockSpec((None, nh, LANE), lambda bi, n: (bi, 0, 0)),
        ],
        out_shape=[
            jax.ShapeDtypeStruct((b * t, mw), odt),
            jax.ShapeDtypeStruct((b, nh, HEAD_DIM, HEAD_DIM), F32),
            jax.ShapeDtypeStruct((b, nh, HEAD_DIM), F32),
            jax.ShapeDtypeStruct((b, nh, LANE), F32),
        ],
        scratch_shapes=[pltpu.VMEM((nh, HEAD_DIM, HEAD_DIM), F32), pltpu.VMEM((nh, HEAD_DIM), F32),
                        pltpu.VMEM((nh, LANE), F32)],
        compiler_params=pltpu.CompilerParams(
            dimension_semantics=("parallel", "arbitrary"), vmem_limit_bytes=VMEM_LIMIT),
        name="mlstm",
    )(big, big, big, big, small, norm_g[None], c0, n0, m0e)
    return o, c_new, n_new, m_new[:, :, 0]


def _l2norm(x):
    return x * lax.rsqrt(jnp.sum(x * x, axis=-1, keepdims=True) + EPS)


def _rmsnorm(x, g):
    return x * lax.rsqrt(jnp.mean(x * x, axis=-1, keepdims=True) + EPS) * g


def _layernorm(x, g, b):
    mu = jnp.mean(x, axis=-1, keepdims=True)
    var = jnp.mean(jnp.square(x - mu), axis=-1, keepdims=True)
    return (x - mu) * lax.rsqrt(var + EPS) * g + b


def _causal_conv(u, state, w):
    t = u.shape[1]
    xp = jnp.concatenate([state, u], axis=1)
    y = xp[:, 0:t] * w[0]
    for j in range(1, CONV_W):
        y = y + xp[:, j:j + t] * w[j]
    return y, xp[:, t:]


def _to_chunks4(x, c):
    b, t, h, d = x.shape
    return x.reshape(b, t // c, c, h, d).transpose(1, 0, 3, 2, 4)


def _to_chunks3(x, c):
    b, t, h = x.shape
    return x.reshape(b, t // c, c, h).transpose(1, 0, 3, 2)


def _from_chunks4(o):
    n, b, h, c, d = o.shape
    return o.transpose(1, 0, 3, 2, 4).reshape(b, n * c, h, d)


def _gated_delta_chunked(q, k, v, beta, g, s0):
    t = q.shape[1]
    c = math.gcd(t, CHUNK)
    qc, kc, vc = _to_chunks4(q, c), _to_chunks4(k, c), _to_chunks4(v, c)
    bc, gc = _to_chunks3(beta, c), _to_chunks3(g, c)
    gcum = jnp.cumsum(gc, axis=-1)
    idx = jnp.arange(c)
    incl = idx[:, None] >= idx[None, :]
    strict = idx[:, None] > idx[None, :]
    diff = gcum[..., :, None] - gcum[..., None, :]
    decay = jnp.where(incl, jnp.exp(jnp.where(incl, diff, 0.0)), 0.0)
    kb = kc * bc[..., None]
    a_mat = jnp.where(strict, jnp.einsum('nbhid,nbhjd->nbhij', kb, kc) * decay, 0.0) + jnp.eye(c, dtype=q.dtype)
    u = lax.linalg.triangular_solve(a_mat, vc * bc[..., None], left_side=True, lower=True, unit_diagonal=True)
    w = lax.linalg.triangular_solve(a_mat, kb * jnp.exp(gcum)[..., None], left_side=True, lower=True,
                                    unit_diagonal=True)
    qk = jnp.einsum('nbhid,nbhjd->nbhij', qc, kc) * decay
    q_dec = qc * jnp.exp(gcum)[..., None]
    k_dec = kc * jnp.exp(gcum[..., -1:] - gcum)[..., None]
    g_last = jnp.exp(gcum[..., -1])

    def step(s, xs):
        u_n, w_n, qk_n, qd_n, kd_n, gl_n = xs
        v_new = u_n - jnp.einsum('bhcd,bhde->bhce', w_n, s)
        o = jnp.einsum('bhcd,bhde->bhce', qd_n, s) + jnp.einsum('bhij,bhje->bhie', qk_n, v_new)
        s = s * gl_n[..., None, None] + jnp.einsum('bhcd,bhce->bhde', kd_n, v_new)
        return s, o

    s, o = lax.scan(step, s0, (u, w, qk, q_dec, k_dec, g_last))
    return _from_chunks4(o), s


def _mlstm_chunked(q, k, v, ig, lf, c0, n0, m0):
    t = q.shape[1]
    c = math.gcd(t, CHUNK)
    qc, kc, vc = _to_chunks4(q, c), _to_chunks4(k, c), _to_chunks4(v, c)
    igc, lfc = _to_chunks3(ig, c), _to_chunks3(lf, c)
    bcum = jnp.cumsum(lfc, axis=-1)
    idx = jnp.arange(c)
    incl = idx[:, None] >= idx[None, :]
    dlog = jnp.where(incl, bcum[..., :, None] - bcum[..., None, :] + igc[..., None, :], -jnp.inf)
    dmax = jnp.max(dlog, axis=-1)
    a_end = bcum[..., -1:] - bcum + igc
    a_max = jnp.max(a_end, axis=-1)
    qk = jnp.einsum('nbhid,nbhjd->nbhij', qc, kc)

    def step(carry, xs):
        cs, ns, ms = carry
        qn, kn, vn, bn, dl, dm, qk_n, an, am = xs
        inter = bn + ms[..., None]
        mt = jnp.maximum(inter, dm)
        sc = qk_n * jnp.exp(dl - mt[..., None])
        w_inter = jnp.exp(inter - mt)
        num = w_inter[..., None] * jnp.einsum('bhcd,bhde->bhce', qn, cs) + jnp.einsum('bhij,bhje->bhie', sc, vn)
        den = w_inter * jnp.einsum('bhcd,bhd->bhc', qn, ns) + jnp.sum(sc, axis=-1)
        h = num / jnp.maximum(jnp.abs(den), jnp.exp(-mt))[..., None]
        m_new = jnp.maximum(bn[..., -1] + ms, am)
        keep = jnp.exp(bn[..., -1] + ms - m_new)
        kw = kn * jnp.exp(an - m_new[..., None])[..., None]
        cs = keep[..., None, None] * cs + jnp.einsum('bhcd,bhce->bhde', kw, vn)
        ns = keep[..., None] * ns + jnp.sum(kw, axis=-2)
        return (cs, ns, m_new), h

    (cs, ns, ms), h = lax.scan(step, (c0, n0, m0), (qc, kc, vc, bcum, dlog, dmax, qk, a_end, a_max))
    return _from_chunks4(h), cs, ns, ms


def _gdn_branch(qkv_raw, z, b_raw, a_raw, conv_w, a_log, dt_bias, norm_g, s0, conv0):
    b, t, _ = qkv_raw.shape
    nh = z.shape[-1] // HEAD_DIM
    conv_out, conv_new = _causal_conv(qkv_raw, conv0, conv_w)
    act = jax.nn.silu(conv_out).reshape(b, t, 3, nh, HEAD_DIM)
    q = _l2norm(act[:, :, 0]) * HEAD_DIM ** -0.5
    k = _l2norm(act[:, :, 1])
    v = act[:, :, 2]
    beta = jax.nn.sigmoid(b_raw)
    g = -jnp.exp(a_log) * jax.nn.softplus(a_raw + dt_bias)
    o, s_new = _gated_delta_chunked(q, k, v, beta, g, s0)
    o = _rmsnorm(o, norm_g) * jax.nn.silu(z.reshape(b, t, nh, HEAD_DIM))
    return o.reshape(b, t, -1), s_new, conv_new


def _mlstm_branch(q_raw, k_raw, v_raw, o_raw, i_raw, f_raw, norm_g, c0, n0, m0):
    b, t, _ = q_raw.shape
    nh = q_raw.shape[-1] // HEAD_DIM
    shp = (b, t, nh, HEAD_DIM)
    q = q_raw.reshape(shp) * HEAD_DIM ** -0.5
    k = k_raw.reshape(shp)
    v = v_raw.reshape(shp)
    lf = jax.nn.log_sigmoid(f_raw)
    h, c_new, n_new, m_new = _mlstm_chunked(q, k, v, i_raw, lf, c0, n0, m0)
    h = _rmsnorm(h, norm_g) * jax.nn.sigmoid(o_raw.reshape(shp))
    return h.reshape(b, t, -1), c_new, n_new, m_new


def _segments(d_model):
    mw = d_model // 2
    nh = mw // HEAD_DIM
    widths = (3 * mw, mw, nh, nh, mw, mw, mw, mw, nh, nh, mw, mw, mw,
              IDX_HEADS * IDX_DIM, IDX_HEADS, IDX_DIM, N_BRANCH * d_model)
    offs = [0]
    for w in widths:
        offs.append(offs[-1] + w)
    names = ('g_qkv', 'g_z', 'g_b', 'g_a', 'm_q', 'm_k', 'm_v', 'm_o', 'm_i', 'm_f',
             'a_q', 'a_k', 'a_v', 'i_q', 'i_w', 'i_k', 'gates')
    return {nm: (offs[i], widths[i]) for i, nm in enumerate(names)}


_BIG = ('g_qkv', 'g_z', 'm_q', 'm_k', 'm_v', 'm_o', 'a_q', 'a_k', 'a_v', 'i_q')
_SMALL = ('i_k', 'g_b', 'g_a', 'm_i', 'm_f', 'i_w')


def _take_cols(a, seg, names):
    return jnp.concatenate([a[..., seg[n][0]:seg[n][0] + seg[n][1]] for n in names], axis=-1)


def _layout(seg, names):
    out, off = {}, 0
    for n in names:
        out[n] = (off, seg[n][1])
        off += seg[n][1]
    return out, off


def _trunk_layer(xs, l, P, rel_bias, sample_ctx, n_prompt, prompt_shape, sample_shape, init_sample):
    d = xs.shape[1]
    mw = d // 2
    nh = mw // HEAD_DIM
    bp, tp = prompt_shape
    bs, ts = sample_shape

    x1 = _ffn(xs, P['norm_g'][l, 0][None], P['ffa_w13'][l], P['ffa_w2'][l])
    g1 = P['norm_g'][l, 1][None]
    big = _inproj(x1, g1, P['w_big'][l], P['b_big'][l][None])
    small, kidx = _inproj_small(x1, g1, P['w_small'][l], P['b_small'][l][None],
                                P['idx_ln_g'][l][None], P['idx_ln_b'][l][None])
    lb, ls = P['lay_big'], P['lay_small']
    att_cols = tuple(lb[n][0] // mw for n in ('a_q', 'a_k', 'a_v'))
    idx_scale = IDX_DIM ** -0.5 * IDX_HEADS ** -0.5

    def seg_b(rows, name):
        o, w = lb[name]
        return rows[:, o:o + w]

    def seg_s(rows, name):
        o, w = ls[name]
        return rows[:, o:o + w]

    outs = []
    states = []
    for grp in range(2):
        if grp == 0:
            rb, rs_, b, t = big[:n_prompt], small[:n_prompt], bp, tp
            init = (jnp.zeros((bp, nh, HEAD_DIM, HEAD_DIM), F32), jnp.zeros((bp, CONV_W - 1, 3 * mw), F32),
                    jnp.zeros((bp, nh, HEAD_DIM, HEAD_DIM), F32), jnp.zeros((bp, nh, HEAD_DIM), F32),
                    jnp.zeros((bp, nh), F32))
        else:
            rb, rs_, b, t = big[n_prompt:], small[n_prompt:], bs, ts
            init = tuple(a[l] for a in init_sample)
        s0, conv0, c0, n0, m0 = init

        def r3(a):
            return a.reshape(b, t, -1)

        row0 = 0 if grp == 0 else n_prompt
        o_a, s_new, conv_new = _gdn(
            big, small, row0, b, t, lb['g_qkv'][0] // (3 * mw), lb['g_z'][0] // mw, ls['g_b'][0], ls['g_a'][0],
            P['gdn_conv_w'][l], P['gdn_a_log'][l], P['gdn_dt_bias'][l], P['gdn_norm_g'][l], s0, conv0)
        o_b, c_new, n_new, m_new = _mlstm(
            big, small, row0, b, t, tuple(lb[nm][0] // mw for nm in ('m_q', 'm_k', 'm_v', 'm_o')),
            ls['m_i'][0], ls['m_f'][0], P['mlstm_norm_g'][l], c0, n0, m0)
        k_att = seg_b(rb, 'a_k').reshape(b, t, nh, HEAD_DIM)
        v_att = seg_b(rb, 'a_v').reshape(b, t, nh, HEAD_DIM)
        k_idx = kidx[row0:row0 + b * t].reshape(b, t, IDX_DIM)
        q_idx = seg_b(rb, 'i_q').reshape(b, t, IDX_HEADS, IDX_DIM).transpose(0, 2, 1, 3).astype(BF16)
        w_idx = r3(seg_s(rs_, 'i_w')) * idx_scale
        if grp == 0:
            o_c = jnp.concatenate([
                _dsa_prompt(big, row0 + bi * t, t, att_cols, q_idx[bi], w_idx[bi], k_idx[bi].astype(BF16),
                            rel_bias) for bi in range(b)], axis=0)
        else:
            cache_k, cache_v, cache_idx_k, page_table = sample_ctx
            o_c = _dsa_sample(big, row0, b, t, att_cols, q_idx.reshape(b, IDX_HEADS * t, IDX_DIM),
                              w_idx.transpose(0, 2, 1).reshape(b, IDX_HEADS * t, 1), k_idx,
                              cache_k.reshape(*cache_k.shape[:2], PAGE_SIZE * nh, HEAD_DIM),
                              cache_v.reshape(*cache_v.shape[:2], PAGE_SIZE * nh, HEAD_DIM),
                              cache_idx_k, l, page_table, rel_bias)
        outs.append((o_a.reshape(b * t, mw).astype(BF16), o_b.reshape(b * t, mw).astype(BF16), o_c))
        states.append((k_att, v_att, k_idx, s_new, conv_new, c_new, n_new, m_new))

    oa = jnp.concatenate([outs[0][0], outs[1][0]], axis=0)
    ob = jnp.concatenate([outs[0][1], outs[1][1]], axis=0)
    oc = jnp.concatenate([outs[0][2], outs[1][2]], axis=0)
    x2 = _merge(x1, g1, oa, ob, oc, P['w_gates'][l], P['b_gates'][l][None], P['w_branch'][l], P['w_out'][l])
    x3 = _ffn(x2, P['norm_g'][l, 2][None], P['ffb_w13'][l], P['ffb_w2'][l])
    return x3, states


def _prep_ffn(w13, w2, tf=512):
    f = w2.shape[1]
    fp = _round_up(f, tf)
    w1 = jnp.pad(w13[:, :, :f], ((0, 0), (0, 0), (0, fp - f)))
    w3 = jnp.pad(w13[:, :, f:], ((0, 0), (0, 0), (0, fp - f)))
    w13p = jnp.concatenate([w1, w3], axis=-1).astype(BF16)
    w2p = jnp.pad(w2, ((0, 0), (0, fp - f), (0, 0))).astype(BF16)
    return w13p, w2p


def kernel(x_prompt, x_sample, cache_k, cache_v, cache_idx_k, state_gdn, state_gdn_conv, state_mlstm_c,
           state_mlstm_n, state_mlstm_m, page_table, norm_g, final_g, ffa_w13, ffa_w2, ffb_w13, ffb_w2,
           w_in, b_in, gdn_conv_w, gdn_a_log, gdn_dt_bias, gdn_norm_g, mlstm_norm_g, idx_ln_g, idx_ln_b,
           rel_bias, w_branch, w_out):
    bp, tp, d = x_prompt.shape
    bs, ts, _ = x_sample.shape
    depth = w_in.shape[0]
    seg = _segments(d)
    lay_big, _ = _layout(seg, _BIG)
    lay_small, n_small = _layout(seg, _SMALL)
    pad_small = _round_up(n_small, LANE) - n_small

    ffa_w13p, ffa_w2p = _prep_ffn(ffa_w13, ffa_w2)
    ffb_w13p, ffb_w2p = _prep_ffn(ffb_w13, ffb_w2)
    go, gw = seg['gates']
    P = dict(
        norm_g=norm_g,
        ffa_w13=ffa_w13p, ffa_w2=ffa_w2p,
        ffb_w13=ffb_w13p, ffb_w2=ffb_w2p,
        w_big=_take_cols(w_in, seg, _BIG).astype(BF16), b_big=_take_cols(b_in, seg, _BIG),
        w_small=jnp.pad(_take_cols(w_in, seg, _SMALL), ((0, 0), (0, 0), (0, pad_small))),
        b_small=jnp.pad(_take_cols(b_in, seg, _SMALL), ((0, 0), (0, pad_small))),
        w_gates=w_in[:, :, go:go + gw].astype(BF16), b_gates=b_in[:, go:go + gw],
        w_branch=w_branch.astype(BF16), w_out=w_out.astype(BF16),
        gdn_conv_w=gdn_conv_w, gdn_a_log=gdn_a_log, gdn_dt_bias=gdn_dt_bias, gdn_norm_g=gdn_norm_g,
        mlstm_norm_g=mlstm_norm_g, idx_ln_g=idx_ln_g, idx_ln_b=idx_ln_b,
        lay_big=lay_big, lay_small=lay_small,
    )

    n_prompt = bp * tp
    xs = jnp.concatenate([x_prompt.reshape(n_prompt, d), x_sample.reshape(bs * ts, d)], axis=0)
    init_sample = (state_gdn, state_gdn_conv, state_mlstm_c, state_mlstm_n, state_mlstm_m)
    sample_ctx = (cache_k, cache_v, cache_idx_k, page_table)
    p_st, s_st = [], []
    for l in range(depth):
        xs, (p, s) = _trunk_layer(xs, l, P, rel_bias, sample_ctx, n_prompt, (bp, tp), (bs, ts), init_sample)
        p_st.append(p)
        s_st.append(s)
    y = _final_norm(xs, final_g[None])
    y_prompt = y[:n_prompt].reshape(bp, tp, d)
    y_sample = y[n_prompt:].reshape(bs, ts, d)

    def stacked(states, i):
        return jnp.stack([s[i] for s in states], axis=0)

    return (y_prompt, y_sample,
            *[stacked(p_st, i) for i in range(8)],
            *[stacked(s_st, i) for i in range(8)])
```

```python
import functools
import math

import jax
import jax.numpy as jnp
from jax import lax
from jax.experimental import pallas as pl
from jax.experimental.pallas import tpu as pltpu

F32 = jnp.float32
BF16 = jnp.bfloat16

HEAD_DIM = 128
CONV_W = 4
CHUNK = 64
IDX_HEADS = 16
IDX_DIM = 64
TOPK_MAX = 256
Q_BLOCK = 128
NUM_BUCKETS = 32
MAX_DISTANCE = 128
PAGE_SIZE = 128
N_BRANCH = 3
EPS = 1e-6

LANE = 128
VMEM_LIMIT = 56 * 1024 * 1024
DSA_TQ = 256


def _round_up(a, b):
    return (a + b - 1) // b * b


def _rms_rows(x, g):
    return x * lax.rsqrt(jnp.mean(x * x, axis=-1, keepdims=True) + EPS) * g


def _ffn_kernel(x_ref, g_ref, w1_ref, w3_ref, w2_ref, o_ref, n_ref):
    j = pl.program_id(1)

    @pl.when(j == 0)
    def _():
        n_ref[...] = _rms_rows(x_ref[...], g_ref[...]).astype(BF16)
        o_ref[...] = jnp.zeros_like(o_ref)

    n = n_ref[...]
    g = jnp.dot(n, w1_ref[...], preferred_element_type=F32)
    u = jnp.dot(n, w3_ref[...], preferred_element_type=F32)
    a = (g * jax.nn.sigmoid(g) * u).astype(BF16)
    o_ref[...] += jnp.dot(a, w2_ref[...], preferred_element_type=F32)

    @pl.when(j == pl.num_programs(1) - 1)
    def _():
        o_ref[...] = x_ref[...] + 0.5 * o_ref[...]


def _ffn(x, g, w13p, w2p, *, tm=512, tf=512):
    m, d = x.shape
    fp = w2p.shape[0]
    nj = fp // tf
    return pl.pallas_call(
        _ffn_kernel,
        grid=(m // tm, nj),
        in_specs=[
            pl.BlockSpec((tm, d), lambda i, j: (i, 0)),
            pl.BlockSpec((1, d), lambda i, j: (0, 0)),
            pl.BlockSpec((d, tf), lambda i, j: (0, j)),
            pl.BlockSpec((d, tf), lambda i, j: (0, j + nj)),
            pl.BlockSpec((tf, d), lambda i, j: (j, 0)),
        ],
        out_specs=pl.BlockSpec((tm, d), lambda i, j: (i, 0)),
        out_shape=jax.ShapeDtypeStruct((m, d), F32),
        scratch_shapes=[pltpu.VMEM((tm, d), BF16)],
        compiler_params=pltpu.CompilerParams(
            dimension_semantics=("parallel", "arbitrary"), vmem_limit_bytes=VMEM_LIMIT),
        name="ffn",
    )(x, g, w13p, w13p, w2p)


def _inproj_kernel(x_ref, g_ref, w_ref, b_ref, o_ref, n_ref):
    @pl.when(pl.program_id(1) == 0)
    def _():
        n_ref[...] = _rms_rows(x_ref[...], g_ref[...]).astype(BF16)

    o_ref[...] = jnp.dot(n_ref[...], w_ref[...], preferred_element_type=F32) + b_ref[...]


def _inproj(x, g, w, b, *, tm=512, tn=1024):
    m, d = x.shape
    n = w.shape[1]
    return pl.pallas_call(
        _inproj_kernel,
        grid=(m // tm, n // tn),
        in_specs=[
            pl.BlockSpec((tm, d), lambda i, j: (i, 0)),
            pl.BlockSpec((1, d), lambda i, j: (0, 0)),
            pl.BlockSpec((d, tn), lambda i, j: (0, j)),
            pl.BlockSpec((1, tn), lambda i, j: (0, j)),
        ],
        out_specs=pl.BlockSpec((tm, tn), lambda i, j: (i, j)),
        out_shape=jax.ShapeDtypeStruct((m, n), F32),
        scratch_shapes=[pltpu.VMEM((tm, d), BF16)],
        compiler_params=pltpu.CompilerParams(
            dimension_semantics=("parallel", "arbitrary"), vmem_limit_bytes=VMEM_LIMIT),
        name="inproj",
    )(x, g, w, b)


def _inproj_small_kernel(x_ref, g_ref, w_ref, b_ref, lg_ref, lb_ref, o_ref, ki_ref):
    n = _rms_rows(x_ref[...], g_ref[...])
    o = jnp.dot(n, w_ref[...], preferred_element_type=F32, precision=lax.Precision.HIGHEST) + b_ref[...]
    o_ref[...] = o
    ik = o[:, :IDX_DIM]
    mu = jnp.mean(ik, axis=-1, keepdims=True)
    var = jnp.mean(jnp.square(ik - mu), axis=-1, keepdims=True)
    ki_ref[...] = (ik - mu) * lax.rsqrt(var + EPS) * lg_ref[...] + lb_ref[...]


def _inproj_small(x, g, w, b, ln_g, ln_b, *, tm=512):
    m, d = x.shape
    n = w.shape[1]
    return pl.pallas_call(
        _inproj_small_kernel,
        grid=(m // tm,),
        in_specs=[
            pl.BlockSpec((tm, d), lambda i: (i, 0)),
            pl.BlockSpec((1, d), lambda i: (0, 0)),
            pl.BlockSpec((d, n), lambda i: (0, 0)),
            pl.BlockSpec((1, n), lambda i: (0, 0)),
            pl.BlockSpec((1, IDX_DIM), lambda i: (0, 0)),
            pl.BlockSpec((1, IDX_DIM), lambda i: (0, 0)),
        ],
        out_specs=[pl.BlockSpec((tm, n), lambda i: (i, 0)), pl.BlockSpec((tm, IDX_DIM), lambda i: (i, 0))],
        out_shape=[jax.ShapeDtypeStruct((m, n), F32), jax.ShapeDtypeStruct((m, IDX_DIM), F32)],
        compiler_params=pltpu.CompilerParams(
            dimension_semantics=("parallel",), vmem_limit_bytes=VMEM_LIMIT),
        name="inproj_small",
    )(x, g, w, b, ln_g, ln_b)


def _merge_kernel(x_ref, g_ref, oa_ref, ob_ref, oc_ref, wg0_ref, wg1_ref, wg2_ref,
                  bg0_ref, bg1_ref, bg2_ref, wb0_ref, wb1_ref, wb2_ref, wo_ref, o_ref, n_ref):
    j = pl.program_id(1)

    @pl.when(j == 0)
    def _():
        n_ref[...] = _rms_rows(x_ref[...], g_ref[...]).astype(BF16)
        o_ref[...] = jnp.zeros_like(o_ref)

    n = n_ref[...]
    merged = None
    for o_r, wg_r, bg_r, wb_r in ((oa_ref, wg0_ref, bg0_ref, wb0_ref),
                                  (ob_ref, wg1_ref, bg1_ref, wb1_ref),
                                  (oc_ref, wg2_ref, bg2_ref, wb2_ref)):
        gate = jax.nn.sigmoid(jnp.dot(n, wg_r[...], preferred_element_type=F32) + bg_r[...])
        y = jnp.dot(o_r[...], wb_r[0], preferred_element_type=F32)
        merged = gate * y if merged is None else merged + gate * y
    o_ref[...] += jnp.dot(merged.astype(BF16), wo_ref[...], preferred_element_type=F32)

    @pl.when(j == pl.num_programs(1) - 1)
    def _():
        o_ref[...] = x_ref[...] + o_ref[...]


def _merge(x, g, oa, ob, oc, wg, bg, wb, wo, *, tm=512, tn=512):
    m, d = x.shape
    w = oa.shape[1]
    nj = d // tn
    wg_specs = [pl.BlockSpec((d, tn), functools.partial(lambda i, j, n: (0, n * nj + j), n=n))
                for n in range(N_BRANCH)]
    bg_specs = [pl.BlockSpec((1, tn), functools.partial(lambda i, j, n: (0, n * nj + j), n=n))
                for n in range(N_BRANCH)]
    wb_specs = [pl.BlockSpec((1, w, tn), functools.partial(lambda i, j, n: (n, 0, j), n=n))
                for n in range(N_BRANCH)]
    o_spec = pl.BlockSpec((tm, w), lambda i, j: (i, 0))
    return pl.pallas_call(
        _merge_kernel,
        grid=(m // tm, nj),
        in_specs=[pl.BlockSpec((tm, d), lambda i, j: (i, 0)),
                  pl.BlockSpec((1, d), lambda i, j: (0, 0)),
                  o_spec, o_spec, o_spec,
                  *wg_specs, *bg_specs, *wb_specs,
                  pl.BlockSpec((tn, d), lambda i, j: (j, 0))],
        out_specs=pl.BlockSpec((tm, d), lambda i, j: (i, 0)),
        out_shape=jax.ShapeDtypeStruct((m, d), F32),
        scratch_shapes=[pltpu.VMEM((tm, d), BF16)],
        compiler_params=pltpu.CompilerParams(
            dimension_semantics=("parallel", "arbitrary"), vmem_limit_bytes=VMEM_LIMIT),
        name="merge",
    )(x, g, oa, ob, oc, wg, wg, wg, bg, bg, bg, wb, wb, wb, wo)


def _norm_kernel(x_ref, g_ref, o_ref):
    o_ref[...] = _rms_rows(x_ref[...], g_ref[...])


def _final_norm(x, g, *, tm=512):
    m, d = x.shape
    return pl.pallas_call(
        _norm_kernel,
        grid=(m // tm,),
        in_specs=[pl.BlockSpec((tm, d), lambda i: (i, 0)), pl.BlockSpec((1, d), lambda i: (0, 0))],
        out_specs=pl.BlockSpec((tm, d), lambda i: (i, 0)),
        out_shape=jax.ShapeDtypeStruct((m, d), F32),
        compiler_params=pltpu.CompilerParams(dimension_semantics=("parallel",)),
        name="final_norm",
    )(x, g)


NEG = -1e30
INT_MIN = -2 ** 31


def _order_key(x):
    bits = lax.bitcast_convert_type(x, jnp.int32)
    return bits ^ ((bits >> 31) & jnp.int32(0x7FFFFFFF))


def _kth_largest_key(count_ge, topk, shape):
    t0 = jnp.where(count_ge(jnp.zeros(shape, jnp.int32)) >= topk, jnp.int32(0), jnp.int32(INT_MIN))

    def bit(b, t):
        cand = t + lax.shift_left(jnp.int32(1), jnp.int32(30) - b)
        return jnp.where(count_ge(cand) >= topk, cand, t)

    return lax.fori_loop(0, 31, bit, t0)


def _kth_largest_key_radix(count_ge, topk, shape, bits):
    t = jnp.where(count_ge(jnp.zeros(shape, jnp.int32)) >= topk, jnp.int32(0), jnp.int32(INT_MIN))
    hi = 31
    while hi > 0:
        nb = hi % bits or bits
        sh = hi - nb
        digit = jnp.zeros(shape, jnp.int32)
        for d in range(1, 2 ** nb):
            digit = digit + jnp.where(count_ge(t + jnp.int32(d << sh)) >= topk, 1, 0)
        t = t + lax.shift_left(digit, jnp.int32(sh))
        hi = sh
    return t


def _softmax_update(s, vb, m_prev, l_prev, acc_prev):
    reps = s.shape[1] // LANE
    m_new = jnp.maximum(m_prev, jnp.max(s, axis=1, keepdims=True))
    alpha = jnp.exp(m_prev - m_new)
    p = jnp.exp(s - jnp.concatenate([m_new] * reps, axis=1))
    l_new = alpha * l_prev + jnp.sum(p, axis=1, keepdims=True)
    pv = jnp.dot(p.astype(BF16), vb, preferred_element_type=F32)
    dreps = pv.shape[1] // LANE
    acc_new = jnp.concatenate([alpha] * dreps, axis=1) * acc_prev + pv
    return m_new, l_new, acc_new


def _dsa_prompt_kernel(it_ref, jt_ref, far_ref, qi_ref, w_ref, ki_ref, q_ref, k_ref, v_ref, bd_ref, bs_ref,
                       o_ref, key_ref, thr_ref, wb_ref, qb_ref, m_ref, l_ref, acc_ref, *, topk, tq, nh):
    i = it_ref[pl.program_id(0)]
    j = jt_ref[pl.program_id(0)]
    rg = 128

    @pl.when(j == 0)
    def _select():
        for h in range(IDX_HEADS):
            wb_ref[h] = jnp.broadcast_to(w_ref[:, h:h + 1], (tq, tq))
        qb_ref[...] = (q_ref[...] * HEAD_DIM ** -0.5).astype(BF16)
        m_ref[...] = jnp.full_like(m_ref, NEG)
        l_ref[...] = jnp.zeros_like(l_ref)
        acc_ref[...] = jnp.zeros_like(acc_ref)
        qi = qi_ref[...].reshape(IDX_HEADS * tq, IDX_DIM)
        row = lax.broadcasted_iota(jnp.int32, (tq, tq), 0)
        col = lax.broadcasted_iota(jnp.int32, (tq, tq), 1)

        def chunk(c, carry):
            kc = ki_ref[pl.ds(pl.multiple_of(c * tq, tq), tq), :]
            s = lax.dot_general(qi, kc, (((1,), (1,)), ((), ())), preferred_element_type=F32)
            sc = jnp.zeros((tq, tq), F32)
            for h in range(IDX_HEADS):
                sc = sc + jnp.maximum(s[h * tq:(h + 1) * tq], 0.0) * wb_ref[h]
            sc = jnp.where(col <= row + jnp.where(c < i, tq, 0), sc, -jnp.inf)
            key_ref[c] = _order_key(sc)
            return carry

        lax.fori_loop(0, i + 1, chunk, 0)

        for r in range(tq // rg):
            rows = pl.ds(r * rg, rg)

            def count_ge(cand):
                def body(c, a):
                    hit = jnp.where(key_ref[c, rows, :] >= cand, 1.0, 0.0)
                    for x in range(tq // LANE):
                        a = a + hit[:, x * LANE:(x + 1) * LANE]
                    return a
                a = lax.fori_loop(0, i + 1, body, jnp.zeros((rg, LANE), F32))
                return jnp.sum(a, axis=1, keepdims=True)

            t = _kth_largest_key(count_ge, float(topk), (rg, 1))
            thr_ref[rows, :] = jnp.broadcast_to(t, (rg, LANE))

    def attend(bias_of_head):
        kk = key_ref[j]
        thr = thr_ref[...]
        selb = jnp.concatenate(
            [jnp.where(kk[:, x * LANE:(x + 1) * LANE] >= thr, 0.0, NEG) for x in range(tq // LANE)], axis=1)
        kb = k_ref[...].astype(BF16)
        vb = v_ref[...].astype(BF16)
        for h in range(nh):
            hs = slice(h * HEAD_DIM, (h + 1) * HEAD_DIM)
            s = lax.dot_general(qb_ref[:, hs], kb[:, hs], (((1,), (1,)), ((), ())),
                                preferred_element_type=F32)
            s = s + selb + bias_of_head(h)
            m_new, l_new, acc_new = _softmax_update(s, vb[:, hs], m_ref[h], l_ref[h], acc_ref[:, hs])
            m_ref[h] = m_new
            l_ref[h] = l_new
            acc_ref[:, hs] = acc_new

    @pl.when(j < i - 1)
    def _far():
        attend(lambda h: far_ref[h])

    @pl.when(j == i - 1)
    def _sub():
        attend(lambda h: bs_ref[h])

    @pl.when(j == i)
    def _diag():
        attend(lambda h: bd_ref[h])
        for h in range(nh):
            hs = slice(h * HEAD_DIM, (h + 1) * HEAD_DIM)
            o_ref[:, hs] = (acc_ref[:, hs] / l_ref[h]).astype(BF16)


def _t5_bucket(dist):
    n = jnp.maximum(dist, 0)
    max_exact = NUM_BUCKETS // 2
    nf = jnp.maximum(n, 1).astype(F32)
    large = max_exact + (jnp.log(nf / max_exact) / math.log(MAX_DISTANCE / max_exact)
                         * (NUM_BUCKETS - max_exact)).astype(jnp.int32)
    large = jnp.minimum(large, NUM_BUCKETS - 1)
    return jnp.where(n < max_exact, n, large)


def _t5_bias(rel_bias, dist):
    onehot = jax.nn.one_hot(_t5_bucket(dist), NUM_BUCKETS, dtype=F32)
    return jnp.einsum('...b,bh->h...', onehot, rel_bias, precision=lax.Precision.HIGHEST)


def _t5_prompt_tiles(rel_bias, tq):
    assert tq >= MAX_DISTANCE
    r = jnp.arange(tq)[:, None]
    c = jnp.arange(tq)[None, :]
    bd = jnp.where((r >= c)[None], _t5_bias(rel_bias, r - c), NEG)
    bs = _t5_bias(rel_bias, tq + r - c)
    far = _t5_bias(rel_bias, jnp.int32(tq + 1))
    return bd, bs, far


def _dsa_prompt(big, row0, t, cols, qi, w, ki, rel_bias, tiles=None, *, tq=DSA_TQ):
    bd, bs, far = _t5_prompt_tiles(rel_bias, tq) if tiles is None else tiles
    nh = rel_bias.shape[1]
    mw = nh * HEAD_DIM
    assert t % tq == 0 and row0 % tq == 0
    nq = t // tq
    rb = row0 // tq
    topk = min(TOPK_MAX, t // 4)
    pairs = [(i, j) for i in range(nq) for j in range(i + 1)]
    it = jnp.asarray([p[0] for p in pairs], jnp.int32)
    jt = jnp.asarray([p[1] for p in pairs], jnp.int32)
    cq, ck, cv = cols
    kern = functools.partial(_dsa_prompt_kernel, topk=topk, tq=tq, nh=nh)
    grid_spec = pltpu.PrefetchScalarGridSpec(
        num_scalar_prefetch=2,
        grid=(len(pairs),),
        in_specs=[
            pl.BlockSpec(memory_space=pltpu.SMEM),
            pl.BlockSpec((IDX_HEADS, tq, IDX_DIM), lambda s, it, jt: (0, it[s], 0)),
            pl.BlockSpec((tq, IDX_HEADS), lambda s, it, jt: (it[s], 0)),
            pl.BlockSpec((t, IDX_DIM), lambda s, it, jt: (0, 0)),
            pl.BlockSpec((tq, mw), lambda s, it, jt: (rb + it[s], cq)),
            pl.BlockSpec((tq, mw), lambda s, it, jt: (rb + jt[s], ck)),
            pl.BlockSpec((tq, mw), lambda s, it, jt: (rb + jt[s], cv)),
            pl.BlockSpec((nh, tq, tq), lambda s, it, jt: (0, 0, 0)),
            pl.BlockSpec((nh, tq, tq), lambda s, it, jt: (0, 0, 0)),
        ],
        out_specs=pl.BlockSpec((tq, mw), lambda s, it, jt: (it[s], 0)),
        scratch_shapes=[
            pltpu.VMEM((nq, tq, tq), jnp.int32),
            pltpu.VMEM((tq, LANE), jnp.int32),
            pltpu.VMEM((IDX_HEADS, tq, tq), F32),
            pltpu.VMEM((tq, mw), BF16),
            pltpu.VMEM((nh, tq, LANE), F32),
            pltpu.VMEM((nh, tq, LANE), F32),
            pltpu.VMEM((tq, mw), F32),
        ],
    )
    return pl.pallas_call(
        kern,
        grid_spec=grid_spec,
        out_shape=jax.ShapeDtypeStruct((t, mw), BF16),
        compiler_params=pltpu.CompilerParams(
            dimension_semantics=("arbitrary",), vmem_limit_bytes=VMEM_LIMIT),
        name="dsa_prompt",
    )(it, jt, far, qi, w, ki, big, big, big, bd, bs)


def _dsa_sample_kernel(pt_ref, qi_ref, w_ref, kin_ref, q_ref, kn_ref, vn_ref, bias_ref, biasn_ref, *rest,
                       topk, t, n_pages, g_pages, nh, past):
    idx_refs = rest[:n_pages]
    kp_refs = rest[n_pages:n_pages + g_pages]
    vp_refs = rest[n_pages + g_pages:n_pages + 2 * g_pages]
    o_ref, selb_ref, selbn_ref, qbd_ref, m_ref, l_ref, acc_ref = rest[n_pages + 2 * g_pages:]
    g = pl.program_id(1)
    ng = n_pages // g_pages
    gw = g_pages * PAGE_SIZE
    rows = nh * t
    mw = nh * HEAD_DIM

    @pl.when(g == 0)
    def _select():
        qi = qi_ref[...]
        wcol = w_ref[...]
        pad = jnp.zeros((PAGE_SIZE - t, IDX_DIM), F32)
        keys = jnp.concatenate([r[...] for r in idx_refs] + [kin_ref[...], pad], axis=0).astype(BF16)
        s = lax.dot_general(qi, keys, (((1,), (1,)), ((), ())), preferred_element_type=F32)
        s = jnp.maximum(s, 0.0) * wcol
        sc = s[0:t]
        for h in range(1, IDX_HEADS):
            sc = sc + s[h * t:(h + 1) * t]
        kpos = lax.broadcasted_iota(jnp.int32, sc.shape, 1)
        qrow = lax.broadcasted_iota(jnp.int32, sc.shape, 0)
        key = _order_key(jnp.where(kpos <= past + qrow, sc, -jnp.inf))

        def count_ge(cand):
            return jnp.sum(jnp.where(key >= cand, 1.0, 0.0), axis=1, keepdims=True)

        thr = _kth_largest_key_radix(count_ge, float(topk), (t, 1), 4)
        selb = jnp.where(key >= thr, 0.0, NEG)
        for x in range(ng):
            selb_ref[x] = selb[:, x * gw:(x + 1) * gw]
        selbn_ref[...] = selb[:, past:]

        qs = jnp.concatenate([q_ref[...] * HEAD_DIM ** -0.5] * nh, axis=0)
        rr = lax.broadcasted_iota(jnp.int32, (rows, mw), 0) >> (t.bit_length() - 1)
        cc = lax.broadcasted_iota(jnp.int32, (rows, mw), 1) >> (HEAD_DIM.bit_length() - 1)
        qbd_ref[...] = jnp.where(rr == cc, qs, 0.0).astype(BF16)
        m_ref[...] = jnp.full_like(m_ref, NEG)
        l_ref[...] = jnp.zeros_like(l_ref)
        acc_ref[...] = jnp.zeros_like(acc_ref)

    def update(kb, vb, bias):
        s = lax.dot_general(qbd_ref[...], kb, (((1,), (1,)), ((), ())), preferred_element_type=F32)
        m_new, l_new, acc_new = _softmax_update(s + bias, vb, m_ref[...], l_ref[...], acc_ref[...])
        m_ref[...] = m_new
        l_ref[...] = l_new
        acc_ref[...] = acc_new

    def page(r):
        return jnp.concatenate([r[pl.ds(h, PAGE_SIZE, stride=nh), :] for h in range(nh)], axis=1)

    kb = jnp.concatenate([page(r) for r in kp_refs], axis=0).astype(BF16)
    vb = jnp.concatenate([page(r) for r in vp_refs], axis=0).astype(BF16)
    update(kb, vb, bias_ref[g] + jnp.concatenate([selb_ref[g]] * nh, axis=0))

    @pl.when(g == ng - 1)
    def _finish():
        padn = jnp.zeros((PAGE_SIZE - t, mw), F32)
        kn = jnp.concatenate([kn_ref[...], padn], axis=0).astype(BF16)
        vn = jnp.concatenate([vn_ref[...], padn], axis=0).astype(BF16)
        update(kn, vn, biasn_ref[...] + jnp.concatenate([selbn_ref[...]] * nh, axis=0))
        for h in range(nh):
            hs = slice(h * HEAD_DIM, (h + 1) * HEAD_DIM)
            rs = slice(h * t, (h + 1) * t)
            o_ref[:, hs] = (acc_ref[rs, hs] / l_ref[rs, :]).astype(BF16)


def _dsa_sample(big, row0, b, t, cols, qi, w, kin, cache_k, cache_v, cache_idx_k, l, page_table, rel_bias):
    nh = rel_bias.shape[1]
    mw = nh * HEAD_DIM
    n_pages = page_table.shape[1]
    past = n_pages * PAGE_SIZE
    g_pages = math.gcd(n_pages, 4)
    ng = n_pages // g_pages
    gw = g_pages * PAGE_SIZE
    topk = min(TOPK_MAX, (past + t) // 4)
    assert row0 % t == 0 and t % 8 == 0 and t <= PAGE_SIZE and t & (t - 1) == 0
    rb = row0 // t
    rows = nh * t
    qpos = past + jnp.arange(t)
    kpos = jnp.arange(past + PAGE_SIZE)
    dist = qpos[:, None] - kpos[None, :]
    bias = jnp.where((dist >= 0)[None], _t5_bias(rel_bias, dist), NEG)
    bias = bias.reshape(rows, past + PAGE_SIZE)
    bias_past = bias[:, :past].reshape(rows, ng, gw).transpose(1, 0, 2)
    bias_new = bias[:, past:]
    cq, ck, cv = cols

    def page_map(x, bi, gi, pt):
        return (l, pt[bi, gi * g_pages + x], 0, 0)

    def idx_map(x, bi, gi, pt):
        return (l, pt[bi, x], 0, 0)

    in_specs = [
        pl.BlockSpec((None, IDX_HEADS * t, IDX_DIM), lambda bi, gi, pt: (bi, 0, 0)),
        pl.BlockSpec((None, IDX_HEADS * t, 1), lambda bi, gi, pt: (bi, 0, 0)),
        pl.BlockSpec((None, t, IDX_DIM), lambda bi, gi, pt: (bi, 0, 0)),
        pl.BlockSpec((t, mw), lambda bi, gi, pt: (rb + bi, cq)),
        pl.BlockSpec((t, mw), lambda bi, gi, pt: (rb + bi, ck)),
        pl.BlockSpec((t, mw), lambda bi, gi, pt: (rb + bi, cv)),
        pl.BlockSpec((ng, rows, gw), lambda bi, gi, pt: (0, 0, 0)),
        pl.BlockSpec((rows, PAGE_SIZE), lambda bi, gi, pt: (0, 0)),
    ]
    in_specs += [pl.BlockSpec((None, None, PAGE_SIZE, IDX_DIM), functools.partial(idx_map, x))
                 for x in range(n_pages)]
    in_specs += [pl.BlockSpec((None, None, PAGE_SIZE * nh, HEAD_DIM), functools.partial(page_map, x % g_pages))
                 for x in range(2 * g_pages)]
    kern = functools.partial(_dsa_sample_kernel, topk=topk, t=t, n_pages=n_pages, g_pages=g_pages,
                             nh=nh, past=past)
    grid_spec = pltpu.PrefetchScalarGridSpec(
        num_scalar_prefetch=1,
        grid=(b, ng),
        in_specs=in_specs,
        out_specs=pl.BlockSpec((t, mw), lambda bi, gi, pt: (bi, 0)),
        scratch_shapes=[
            pltpu.VMEM((ng, t, gw), F32),
            pltpu.VMEM((t, PAGE_SIZE), F32),
            pltpu.VMEM((rows, mw), BF16),
            pltpu.VMEM((rows, LANE), F32),
            pltpu.VMEM((rows, LANE), F32),
            pltpu.VMEM((rows, mw), F32),
        ],
    )
    return pl.pallas_call(
        kern,
        grid_spec=grid_spec,
        out_shape=jax.ShapeDtypeStruct((b * t, mw), BF16),
        compiler_params=pltpu.CompilerParams(
            dimension_semantics=("parallel", "arbitrary"), vmem_limit_bytes=VMEM_LIMIT),
        name="dsa_sample",
    )(page_table, qi, w, kin, big, big, big, bias_past, bias_new,
      *([cache_idx_k] * n_pages), *([cache_k] * g_pages), *([cache_v] * g_pages))


def _lane_expand(x, nh):
    c = x.shape[0]
    return jnp.concatenate([jnp.broadcast_to(x[:, h:h + 1], (c, HEAD_DIM)) for h in range(nh)], axis=1)


def _bdot(a, b):
    return jnp.dot(a.astype(BF16), b.astype(BF16), preferred_element_type=F32)


def _bdot_nt(a, b):
    return lax.dot_general(a.astype(BF16), b.astype(BF16), (((1,), (1,)), ((), ())), preferred_element_type=F32)


def _bdot_tn(a, b):
    return lax.dot_general(a.astype(BF16), b.astype(BF16), (((0,), (0,)), ((), ())), preferred_element_type=F32)


def _softplus(x):
    return jnp.maximum(x, 0.0) + jnp.log(1.0 + jnp.exp(-jnp.abs(x)))


def _gdn_kernel(qkv_ref, z_ref, sm_ref, cw_ref, alog_ref, dtb_ref, ng_ref, s0_ref, conv0_ref,
                o_ref, s_out_ref, conv_out_ref, s_ref, xp_ref, *, c, nh, ob, oa):
    n = pl.program_id(1)
    mw = nh * HEAD_DIM
    halo = 8

    @pl.when(n == 0)
    def _():
        s_ref[...] = s0_ref[...]
        xp_ref[0:halo, :] = conv0_ref[...]

    xp_ref[halo:halo + c, :] = qkv_ref[...]
    y = cw_ref[0:1, :] * xp_ref[halo - CONV_W + 1:halo - CONV_W + 1 + c, :]
    for j in range(1, CONV_W):
        y = y + cw_ref[j:j + 1, :] * xp_ref[halo - CONV_W + 1 + j:halo - CONV_W + 1 + j + c, :]
    act = y * jax.nn.sigmoid(y)

    sm = sm_ref[...]
    beta = jax.nn.sigmoid(sm[:, ob:ob + nh])
    g = -jnp.exp(alog_ref[...]) * _softplus(sm[:, oa:oa + nh] + dtb_ref[...])
    beta_e = _lane_expand(beta, nh)
    g_e = _lane_expand(g, nh)
    t_i = lax.broadcasted_iota(jnp.int32, (c, mw), 0)
    j_i = lax.broadcasted_iota(jnp.int32, (c, mw), 1) & (HEAD_DIM - 1)
    ii = lax.broadcasted_iota(jnp.int32, (c, c), 0)
    jj = lax.broadcasted_iota(jnp.int32, (c, c), 1)
    tri = jnp.where(jj <= ii, 1.0, 0.0)
    cum = jnp.dot(tri, jnp.concatenate([g_e, jnp.where(t_i > j_i, g_e, 0.0)], axis=1),
                  preferred_element_type=F32, precision=lax.Precision.HIGHEST)
    gc = cum[:, :mw]
    dmat = cum[:, mw:]
    g_end = gc[c - 1:c, :]
    e_gc = jnp.exp(gc)
    e_rest = jnp.exp(g_end - gc)
    e_end = jnp.exp(g_end)
    eye = jnp.where(ii == jj, 1.0, 0.0)

    heads = range(nh)
    hsl = [slice(h * HEAD_DIM, (h + 1) * HEAD_DIM) for h in heads]
    qn, kn, kb, dec = [], [], [], []
    for h in heads:
        qh = act[:, h * HEAD_DIM:(h + 1) * HEAD_DIM]
        kh = act[:, mw + h * HEAD_DIM:mw + (h + 1) * HEAD_DIM]
        qn.append(qh * lax.rsqrt(jnp.sum(qh * qh, axis=1, keepdims=True) + EPS) * HEAD_DIM ** -0.5)
        kn.append(kh * lax.rsqrt(jnp.sum(kh * kh, axis=1, keepdims=True) + EPS))
        kb.append(kn[h] * beta_e[:, hsl[h]])
        dec.append(jnp.where(ii >= jj, jnp.exp(dmat[:, h * HEAD_DIM:h * HEAD_DIM + c]), 0.0))
    knb = [kn[h].astype(BF16) for h in heads]
    kk = [_bdot_nt(kb[h], knb[h]) for h in heads]
    qk = [_bdot_nt(qn[h], knb[h]) for h in heads]
    x = [jnp.where(ii > jj, -kk[h] * dec[h], 0.0) for h in heads]
    p = [eye + x[h] for h in heads]
    for _ in range(c.bit_length() - 2):
        x = [_bdot(x[h], x[h]) for h in heads]
        p = [p[h] + _bdot(p[h], x[h]) for h in heads]
    uw = [_bdot(p[h], jnp.concatenate(
        [act[:, 2 * mw + h * HEAD_DIM:2 * mw + (h + 1) * HEAD_DIM] * beta_e[:, hsl[h]],
         kb[h] * e_gc[:, hsl[h]]], axis=1)) for h in heads]
    s_old = [s_ref[h] for h in heads]
    sb = [s_old[h].astype(BF16) for h in heads]
    o_inter = [_bdot(qn[h] * e_gc[:, hsl[h]], sb[h]) for h in heads]
    v_new = [uw[h][:, :HEAD_DIM] - _bdot(uw[h][:, HEAD_DIM:], sb[h]) for h in heads]
    o_intra = [_bdot(qk[h] * dec[h], v_new[h]) for h in heads]
    kv = [_bdot_tn(kn[h] * e_rest[:, hsl[h]], v_new[h]) for h in heads]
    for h in heads:
        s_ref[h] = s_old[h] * e_end[:, hsl[h]] + kv[h]
        o = o_inter[h] + o_intra[h]
        on = o * lax.rsqrt(jnp.mean(o * o, axis=1, keepdims=True) + EPS) * ng_ref[...]
        zh = z_ref[:, hsl[h]]
        o_ref[:, hsl[h]] = (on * zh * jax.nn.sigmoid(zh)).astype(o_ref.dtype)

    tail = xp_ref[c:c + halo, :]
    xp_ref[0:halo, :] = tail

    @pl.when(n == pl.num_programs(1) - 1)
    def _():
        s_out_ref[...] = s_ref[...]
        conv_out_ref[...] = tail


def _gdn(big, small, row0, b, t, cq, cz, ob, oa, conv_w, a_log, dt_bias, norm_g, s0, conv0):
    nh = s0.shape[1]
    mw = nh * HEAD_DIM
    c = math.gcd(t, CHUNK)
    assert c % 8 == 0 and c & (c - 1) == 0 and row0 % c == 0
    nc = t // c
    rb = row0 // c
    halo = 8
    conv0p = jnp.pad(conv0, ((0, 0), (halo - (CONV_W - 1), 0), (0, 0)))
    odt = BF16 if c % 16 == 0 else F32
    kern = functools.partial(_gdn_kernel, c=c, nh=nh, ob=ob, oa=oa)
    o, s_new, conv_new = pl.pallas_call(
        kern,
        grid=(b, nc),
        in_specs=[
            pl.BlockSpec((c, 3 * mw), lambda bi, n: (rb + bi * nc + n, cq)),
            pl.BlockSpec((c, mw), lambda bi, n: (rb + bi * nc + n, cz)),
            pl.BlockSpec((c, LANE), lambda bi, n: (rb + bi * nc + n, 0)),
            pl.BlockSpec((CONV_W, 3 * mw), lambda bi, n: (0, 0)),
            pl.BlockSpec((1, nh), lambda bi, n: (0, 0)),
            pl.BlockSpec((1, nh), lambda bi, n: (0, 0)),
            pl.BlockSpec((1, HEAD_DIM), lambda bi, n: (0, 0)),
            pl.BlockSpec((None, nh, HEAD_DIM, HEAD_DIM), lambda bi, n: (bi, 0, 0, 0)),
            pl.BlockSpec((None, halo, 3 * mw), lambda bi, n: (bi, 0, 0)),
        ],
        out_specs=[
            pl.BlockSpec((c, mw), lambda bi, n: (bi * nc + n, 0)),
            pl.BlockSpec((None, nh, HEAD_DIM, HEAD_DIM), lambda bi, n: (bi, 0, 0, 0)),
            pl.BlockSpec((None, halo, 3 * mw), lambda bi, n: (bi, 0, 0)),
        ],
        out_shape=[
            jax.ShapeDtypeStruct((b * t, mw), odt),
            jax.ShapeDtypeStruct((b, nh, HEAD_DIM, HEAD_DIM), F32),
            jax.ShapeDtypeStruct((b, halo, 3 * mw), F32),
        ],
        scratch_shapes=[pltpu.VMEM((nh, HEAD_DIM, HEAD_DIM), F32), pltpu.VMEM((halo + c, 3 * mw), F32)],
        compiler_params=pltpu.CompilerParams(
            dimension_semantics=("parallel", "arbitrary"), vmem_limit_bytes=VMEM_LIMIT),
        name="gdn",
    )(big, big, small, conv_w, a_log[None], dt_bias[None], norm_g[None], s0, conv0p)
    return o, s_new, conv_new[:, halo - (CONV_W - 1):]


def _mlstm_kernel(q_ref, k_ref, v_ref, og_ref, sm_ref, ng_ref, c0_ref, n0_ref, m0_ref,
                  o_ref, c_out_ref, n_out_ref, m_out_ref, c_ref, n_ref, m_ref, *, c, nh, oi, of):
    n = pl.program_id(1)
    mw = nh * HEAD_DIM

    @pl.when(n == 0)
    def _():
        c_ref[...] = c0_ref[...]
        n_ref[...] = n0_ref[...]
        m_ref[...] = m0_ref[...]

    sm = sm_ref[...]
    ig_e = _lane_expand(sm[:, oi:oi + nh], nh)
    lf_e = _lane_expand(-_softplus(-sm[:, of:of + nh]), nh)
    t_i = lax.broadcasted_iota(jnp.int32, (c, mw), 0)
    j_i = lax.broadcasted_iota(jnp.int32, (c, mw), 1) & (HEAD_DIM - 1)
    ii = lax.broadcasted_iota(jnp.int32, (c, c), 0)
    jj = lax.broadcasted_iota(jnp.int32, (c, c), 1)
    lhs = jnp.concatenate([jnp.where(jj <= ii, 1.0, 0.0), jnp.ones((c, c), F32)], axis=1)
    rhs = jnp.concatenate([
        jnp.concatenate([lf_e, jnp.where(t_i > j_i, lf_e, 0.0)], axis=1),
        jnp.concatenate([jnp.zeros((c, mw), F32), jnp.where(t_i == j_i, ig_e, 0.0)], axis=1)], axis=0)
    cum = jnp.dot(lhs, rhs, preferred_element_type=F32, precision=lax.Precision.HIGHEST)
    bc = cum[:, :mw]
    dlog = cum[:, mw:]
    b_end = bc[c - 1:c, :]
    a_end = b_end - bc + ig_e

    heads = range(nh)
    hsl = [slice(h * HEAD_DIM, (h + 1) * HEAD_DIM) for h in heads]
    qb = [(q_ref[:, hsl[h]] * HEAD_DIM ** -0.5).astype(BF16) for h in heads]
    kh = [k_ref[:, hsl[h]] for h in heads]
    vb = [v_ref[:, hsl[h]].astype(BF16) for h in heads]
    cs = [c_ref[h] for h in heads]
    qk = [_bdot_nt(qb[h], kh[h]) for h in heads]
    qc = [_bdot(qb[h], cs[h]) for h in heads]
    m_prev = [m_ref[h:h + 1, :] for h in heads]
    m_new = [jnp.maximum(b_end[:, hsl[h]] + m_prev[h], jnp.max(a_end[:, hsl[h]], axis=0, keepdims=True))
             for h in heads]
    kw = [kh[h] * jnp.exp(a_end[:, hsl[h]] - m_new[h]) for h in heads]
    kv = [_bdot_tn(kw[h], vb[h]) for h in heads]
    mt, sc = [], []
    for h in heads:
        dl = jnp.where(ii >= jj, dlog[:, h * HEAD_DIM:h * HEAD_DIM + c], -jnp.inf)
        inter = bc[:, hsl[h]] + m_prev[h]
        mt.append(jnp.maximum(inter, jnp.max(dl, axis=1, keepdims=True)))
        sc.append(qk[h] * jnp.exp(dl - mt[h][:, :c]))
    sv = [_bdot(sc[h], vb[h]) for h in heads]
    for h in heads:
        w_inter = jnp.exp(bc[:, hsl[h]] + m_prev[h] - mt[h])
        ns = n_ref[h:h + 1, :]
        qf = q_ref[:, hsl[h]] * HEAD_DIM ** -0.5
        num = w_inter * qc[h] + sv[h]
        den = w_inter * jnp.sum(qf * ns, axis=1, keepdims=True) + jnp.sum(sc[h], axis=1, keepdims=True)
        hh = num / jnp.maximum(jnp.abs(den), jnp.exp(-mt[h]))
        keep = jnp.exp(b_end[:, hsl[h]] + m_prev[h] - m_new[h])
        c_ref[h] = keep * cs[h] + kv[h]
        n_ref[h:h + 1, :] = keep * ns + jnp.sum(kw[h], axis=0, keepdims=True)
        m_ref[h:h + 1, :] = m_new[h]
        hn = hh * lax.rsqrt(jnp.mean(hh * hh, axis=1, keepdims=True) + EPS) * ng_ref[...]
        o_ref[:, hsl[h]] = (hn * jax.nn.sigmoid(og_ref[:, hsl[h]])).astype(o_ref.dtype)

    @pl.when(n == pl.num_programs(1) - 1)
    def _():
        c_out_ref[...] = c_ref[...]
        n_out_ref[...] = n_ref[...]
        m_out_ref[...] = m_ref[...]


def _mlstm(big, small, row0, b, t, cols, oi, of, norm_g, c0, n0, m0):
    nh = c0.shape[1]
    mw = nh * HEAD_DIM
    c = math.gcd(t, CHUNK)
    assert c % 8 == 0 and c <= HEAD_DIM and row0 % c == 0
    nc = t // c
    rb = row0 // c
    odt = BF16 if c % 16 == 0 else F32
    m0e = jnp.broadcast_to(m0[:, :, None], (b, nh, LANE))

    def col_spec(cc):
        return pl.BlockSpec((c, mw), lambda bi, n: (rb + bi * nc + n, cc))

    kern = functools.partial(_mlstm_kernel, c=c, nh=nh, oi=oi, of=of)
    o, c_new, n_new, m_new = pl.pallas_call(
        kern,
        grid=(b, nc),
        in_specs=[
            *[col_spec(cc) for cc in cols],
            pl.BlockSpec((c, LANE), lambda bi, n: (rb + bi * nc + n, 0)),
            pl.BlockSpec((1, HEAD_DIM), lambda bi, n: (0, 0)),
            pl.BlockSpec((None, nh, HEAD_DIM, HEAD_DIM), lambda bi, n: (bi, 0, 0, 0)),
            pl.BlockSpec((None, nh, HEAD_DIM), lambda bi, n: (bi, 0, 0)),
            pl.BlockSpec((None, nh, LANE), lambda bi, n: (bi, 0, 0)),
        ],
        out_specs=[
            pl.BlockSpec((c, mw), lambda bi, n: (bi * nc + n, 0)),
            pl.BlockSpec((None, nh, HEAD_DIM, HEAD_DIM), lambda bi, n: (bi, 0, 0, 0)),
            pl.BlockSpec((None, nh, HEAD_DIM), lambda bi, n: (bi, 0, 0)),
            pl.BlockSpec((None, nh, LANE), lambda bi, n: (bi, 0, 0)),
        ],
        out_shape=[
            jax.ShapeDtypeStruct((b * t, mw), odt),
            jax.ShapeDtypeStruct((b, nh, HEAD_DIM, HEAD_DIM), F32),
            jax.ShapeDtypeStruct((b, nh, HEAD_DIM), F32),
            jax.ShapeDtypeStruct((b, nh, LANE), F32),
        ],
        scratch_shapes=[pltpu.VMEM((nh, HEAD_DIM, HEAD_DIM), F32), pltpu.VMEM((nh, HEAD_DIM), F32),
                        pltpu.VMEM((nh, LANE), F32)],
        compiler_params=pltpu.CompilerParams(
            dimension_semantics=("parallel", "arbitrary"), vmem_limit_bytes=VMEM_LIMIT),
        name="mlstm",
    )(big, big, big, big, small, norm_g[None], c0, n0, m0e)
    return o, c_new, n_new, m_new[:, :, 0]


def _segments(d_model):
    mw = d_model // 2
    nh = mw // HEAD_DIM
    widths = (3 * mw, mw, nh, nh, mw, mw, mw, mw, nh, nh, mw, mw, mw,
              IDX_HEADS * IDX_DIM, IDX_HEADS, IDX_DIM, N_BRANCH * d_model)
    offs = [0]
    for w in widths:
        offs.append(offs[-1] + w)
    names = ('g_qkv', 'g_z', 'g_b', 'g_a', 'm_q', 'm_k', 'm_v', 'm_o', 'm_i', 'm_f',
             'a_q', 'a_k', 'a_v', 'i_q', 'i_w', 'i_k', 'gates')
    return {nm: (offs[i], widths[i]) for i, nm in enumerate(names)}


_BIG = ('g_qkv', 'g_z', 'm_q', 'm_k', 'm_v', 'm_o', 'a_q', 'a_k', 'a_v', 'i_q')
_SMALL = ('i_k', 'g_b', 'g_a', 'm_i', 'm_f', 'i_w')


def _take_cols(a, seg, names):
    return jnp.concatenate([a[..., seg[n][0]:seg[n][0] + seg[n][1]] for n in names], axis=-1)


def _layout(seg, names):
    out, off = {}, 0
    for n in names:
        out[n] = (off, seg[n][1])
        off += seg[n][1]
    return out, off


def _trunk_layer(xs, l, P, rel_bias, sample_ctx, n_prompt, prompt_shape, sample_shape, init_sample):
    d = xs.shape[1]
    mw = d // 2
    nh = mw // HEAD_DIM
    bp, tp = prompt_shape
    bs, ts = sample_shape

    x1 = _ffn(xs, P['norm_g'][l, 0][None], P['ffa_w13'][l], P['ffa_w2'][l])
    g1 = P['norm_g'][l, 1][None]
    big = _inproj(x1, g1, P['w_big'][l], P['b_big'][l][None])
    small, kidx = _inproj_small(x1, g1, P['w_small'][l], P['b_small'][l][None],
                                P['idx_ln_g'][l][None], P['idx_ln_b'][l][None])
    lb, ls = P['lay_big'], P['lay_small']
    att_cols = tuple(lb[n][0] // mw for n in ('a_q', 'a_k', 'a_v'))
    idx_scale = IDX_DIM ** -0.5 * IDX_HEADS ** -0.5

    outs = []
    states = []
    for grp in range(2):
        if grp == 0:
            b, t = bp, tp
            init = (jnp.zeros((bp, nh, HEAD_DIM, HEAD_DIM), F32), jnp.zeros((bp, CONV_W - 1, 3 * mw), F32),
                    jnp.zeros((bp, nh, HEAD_DIM, HEAD_DIM), F32), jnp.zeros((bp, nh, HEAD_DIM), F32),
                    jnp.zeros((bp, nh), F32))
        else:
            b, t = bs, ts
            init = tuple(a[l] for a in init_sample)
        s0, conv0, c0, n0, m0 = init
        row0 = 0 if grp == 0 else n_prompt

        def seg_b(name):
            o, w = lb[name]
            return lax.slice(big, (row0, o), (row0 + b * t, o + w))

        def seg_s(name):
            o, w = ls[name]
            return lax.slice(small, (row0, o), (row0 + b * t, o + w))

        o_a, s_new, conv_new = _gdn(
            big, small, row0, b, t, lb['g_qkv'][0] // (3 * mw), lb['g_z'][0] // mw, ls['g_b'][0], ls['g_a'][0],
            P['gdn_conv_w'][l], P['gdn_a_log'][l], P['gdn_dt_bias'][l], P['gdn_norm_g'][l], s0, conv0)
        o_b, c_new, n_new, m_new = _mlstm(
            big, small, row0, b, t, tuple(lb[nm][0] // mw for nm in ('m_q', 'm_k', 'm_v', 'm_o')),
            ls['m_i'][0], ls['m_f'][0], P['mlstm_norm_g'][l], c0, n0, m0)
        k_att = seg_b('a_k').reshape(b, t, nh, HEAD_DIM)
        v_att = seg_b('a_v').reshape(b, t, nh, HEAD_DIM)
        k_idx = kidx[row0:row0 + b * t].reshape(b, t, IDX_DIM)
        q_idx = seg_b('i_q').reshape(b, t, IDX_HEADS, IDX_DIM).transpose(0, 2, 1, 3).astype(BF16)
        w_idx = seg_s('i_w').reshape(b, t, IDX_HEADS) * idx_scale
        if grp == 0:
            o_c = jnp.concatenate([
                _dsa_prompt(big, row0 + bi * t, t, att_cols, q_idx[bi], w_idx[bi], k_idx[bi].astype(BF16),
                            rel_bias, P['t5_tiles']) for bi in range(b)], axis=0)
        else:
            cache_k, cache_v, cache_idx_k, page_table = sample_ctx
            o_c = _dsa_sample(big, row0, b, t, att_cols, q_idx.reshape(b, IDX_HEADS * t, IDX_DIM),
                              w_idx.transpose(0, 2, 1).reshape(b, IDX_HEADS * t, 1), k_idx,
                              cache_k.reshape(*cache_k.shape[:2], PAGE_SIZE * nh, HEAD_DIM),
                              cache_v.reshape(*cache_v.shape[:2], PAGE_SIZE * nh, HEAD_DIM),
                              cache_idx_k, l, page_table, rel_bias)
        outs.append((o_a.reshape(b * t, mw).astype(BF16), o_b.reshape(b * t, mw).astype(BF16), o_c))
        states.append((k_att, v_att, k_idx, s_new, conv_new, c_new, n_new, m_new))

    oa = jnp.concatenate([outs[0][0], outs[1][0]], axis=0)
    ob = jnp.concatenate([outs[0][1], outs[1][1]], axis=0)
    oc = jnp.concatenate([outs[0][2], outs[1][2]], axis=0)
    x2 = _merge(x1, g1, oa, ob, oc, P['w_gates'][l], P['b_gates'][l][None], P['w_branch'][l], P['w_out'][l])
    x3 = _ffn(x2, P['norm_g'][l, 2][None], P['ffb_w13'][l], P['ffb_w2'][l])
    return x3, states


def _repack_kernel(x_ref, o_ref, *, nb_valid, nb_padded):
    jb = pl.program_id(1) % nb_padded
    o_ref[...] = jnp.where(jb < nb_valid, x_ref[...], 0.0).astype(o_ref.dtype)


def _repack_bf16(w, axis, f, fp):
    nl, a, b = w.shape
    groups = w.shape[axis] // f
    assert f % LANE == 0 and fp % LANE == 0 and groups * f == w.shape[axis]
    nbv, nbp = f // LANE, fp // LANE

    def src_block(j):
        return (j // nbp) * nbv + jnp.minimum(j % nbp, nbv - 1)

    if axis == 2:
        in_spec = pl.BlockSpec((None, a, LANE), lambda l, j: (l, 0, src_block(j)))
        out_spec = pl.BlockSpec((None, a, LANE), lambda l, j: (l, 0, j))
        out_shape = (nl, a, groups * fp)
    else:
        in_spec = pl.BlockSpec((None, LANE, b), lambda l, j: (l, src_block(j), 0))
        out_spec = pl.BlockSpec((None, LANE, b), lambda l, j: (l, j, 0))
        out_shape = (nl, groups * fp, b)
    return pl.pallas_call(
        functools.partial(_repack_kernel, nb_valid=nbv, nb_padded=nbp),
        grid=(nl, groups * nbp),
        in_specs=[in_spec],
        out_specs=out_spec,
        out_shape=jax.ShapeDtypeStruct(out_shape, BF16),
        compiler_params=pltpu.CompilerParams(dimension_semantics=("parallel", "arbitrary")),
        name="repack",
    )(w)


def _prep_ffn(w13, w2, tf=512):
    f = w2.shape[1]
    fp = _round_up(f, tf)
    return _repack_bf16(w13, 2, f, fp), _repack_bf16(w2, 1, f, fp)


def kernel(x_prompt, x_sample, cache_k, cache_v, cache_idx_k, state_gdn, state_gdn_conv, state_mlstm_c,
           state_mlstm_n, state_mlstm_m, page_table, norm_g, final_g, ffa_w13, ffa_w2, ffb_w13, ffb_w2,
           w_in, b_in, gdn_conv_w, gdn_a_log, gdn_dt_bias, gdn_norm_g, mlstm_norm_g, idx_ln_g, idx_ln_b,
           rel_bias, w_branch, w_out):
    bp, tp, d = x_prompt.shape
    bs, ts, _ = x_sample.shape
    depth = w_in.shape[0]
    seg = _segments(d)
    lay_big, _ = _layout(seg, _BIG)
    lay_small, n_small = _layout(seg, _SMALL)
    pad_small = _round_up(n_small, LANE) - n_small

    ffa_w13p, ffa_w2p = _prep_ffn(ffa_w13, ffa_w2)
    ffb_w13p, ffb_w2p = _prep_ffn(ffb_w13, ffb_w2)
    go, gw = seg['gates']
    P = dict(
        norm_g=norm_g,
        ffa_w13=ffa_w13p, ffa_w2=ffa_w2p,
        ffb_w13=ffb_w13p, ffb_w2=ffb_w2p,
        w_big=_take_cols(w_in, seg, _BIG).astype(BF16), b_big=_take_cols(b_in, seg, _BIG),
        w_small=jnp.pad(_take_cols(w_in, seg, _SMALL), ((0, 0), (0, 0), (0, pad_small))),
        b_small=jnp.pad(_take_cols(b_in, seg, _SMALL), ((0, 0), (0, pad_small))),
        w_gates=w_in[:, :, go:go + gw].astype(BF16), b_gates=b_in[:, go:go + gw],
        w_branch=w_branch.astype(BF16), w_out=w_out.astype(BF16),
        gdn_conv_w=gdn_conv_w, gdn_a_log=gdn_a_log, gdn_dt_bias=gdn_dt_bias, gdn_norm_g=gdn_norm_g,
        mlstm_norm_g=mlstm_norm_g, idx_ln_g=idx_ln_g, idx_ln_b=idx_ln_b,
        lay_big=lay_big, lay_small=lay_small,
        t5_tiles=_t5_prompt_tiles(rel_bias, DSA_TQ),
    )

    n_prompt = bp * tp
    xs = jnp.concatenate([x_prompt.reshape(n_prompt, d), x_sample.reshape(bs * ts, d)], axis=0)
    init_sample = (state_gdn, state_gdn_conv, state_mlstm_c, state_mlstm_n, state_mlstm_m)
    sample_ctx = (cache_k, cache_v, cache_idx_k, page_table)
    p_st, s_st = [], []
    for l in range(depth):
        xs, (p, s) = _trunk_layer(xs, l, P, rel_bias, sample_ctx, n_prompt, (bp, tp), (bs, ts), init_sample)
        p_st.append(p)
        s_st.append(s)
    y = _final_norm(xs, final_g[None])
    y_prompt = y[:n_prompt].reshape(bp, tp, d)
    y_sample = y[n_prompt:].reshape(bs, ts, d)

    def stacked(states, i):
        return jnp.stack([s[i] for s in states], axis=0)

    return (y_prompt, y_sample,
            *[stacked(p_st, i) for i in range(8)],
            *[stacked(s_st, i) for i in range(8)])
```

```python
import functools
import math

import jax
import jax.numpy as jnp
from jax import lax
from jax.experimental import pallas as pl
from jax.experimental.pallas import tpu as pltpu

F32 = jnp.float32
BF16 = jnp.bfloat16

HEAD_DIM = 128
CONV_W = 4
CHUNK = 64
IDX_HEADS = 16
IDX_DIM = 64
TOPK_MAX = 256
Q_BLOCK = 128
NUM_BUCKETS = 32
MAX_DISTANCE = 128
PAGE_SIZE = 128
N_BRANCH = 3
EPS = 1e-6

LANE = 128
VMEM_LIMIT = 56 * 1024 * 1024
DSA_TQ = 256


def _round_up(a, b):
    return (a + b - 1) // b * b


def _rms_rows(x, g):
    return x * lax.rsqrt(jnp.mean(x * x, axis=-1, keepdims=True) + EPS) * g


def _ffn_kernel(x_ref, g_ref, w1_ref, w3_ref, w2_ref, o_ref, n_ref):
    j = pl.program_id(1)

    @pl.when(j == 0)
    def _():
        n_ref[...] = _rms_rows(x_ref[...], g_ref[...]).astype(BF16)
        o_ref[...] = jnp.zeros_like(o_ref)

    n = n_ref[...]
    g = jnp.dot(n, w1_ref[...], preferred_element_type=F32)
    u = jnp.dot(n, w3_ref[...], preferred_element_type=F32)
    a = (g * jax.nn.sigmoid(g) * u).astype(BF16)
    o_ref[...] += jnp.dot(a, w2_ref[...], preferred_element_type=F32)

    @pl.when(j == pl.num_programs(1) - 1)
    def _():
        o_ref[...] = x_ref[...] + 0.5 * o_ref[...]


def _ffn(x, g, w13p, w2p, *, tm=512, tf=512):
    m, d = x.shape
    fp = w2p.shape[0]
    nj = fp // tf
    return pl.pallas_call(
        _ffn_kernel,
        grid=(m // tm, nj),
        in_specs=[
            pl.BlockSpec((tm, d), lambda i, j: (i, 0)),
            pl.BlockSpec((1, d), lambda i, j: (0, 0)),
            pl.BlockSpec((d, tf), lambda i, j: (0, j)),
            pl.BlockSpec((d, tf), lambda i, j: (0, j + nj)),
            pl.BlockSpec((tf, d), lambda i, j: (j, 0)),
        ],
        out_specs=pl.BlockSpec((tm, d), lambda i, j: (i, 0)),
        out_shape=jax.ShapeDtypeStruct((m, d), F32),
        scratch_shapes=[pltpu.VMEM((tm, d), BF16)],
        compiler_params=pltpu.CompilerParams(
            dimension_semantics=("parallel", "arbitrary"), vmem_limit_bytes=VMEM_LIMIT),
        name="ffn",
    )(x, g, w13p, w13p, w2p)


def _inproj_kernel(x_ref, g_ref, w_ref, b_ref, o_ref, n_ref):
    @pl.when(pl.program_id(1) == 0)
    def _():
        n_ref[...] = _rms_rows(x_ref[...], g_ref[...]).astype(BF16)

    o_ref[...] = jnp.dot(n_ref[...], w_ref[...], preferred_element_type=F32) + b_ref[...]


def _inproj(x, g, w, b, *, tm=512, tn=1024):
    m, d = x.shape
    n = w.shape[1]
    return pl.pallas_call(
        _inproj_kernel,
        grid=(m // tm, n // tn),
        in_specs=[
            pl.BlockSpec((tm, d), lambda i, j: (i, 0)),
            pl.BlockSpec((1, d), lambda i, j: (0, 0)),
            pl.BlockSpec((d, tn), lambda i, j: (0, j)),
            pl.BlockSpec((1, tn), lambda i, j: (0, j)),
        ],
        out_specs=pl.BlockSpec((tm, tn), lambda i, j: (i, j)),
        out_shape=jax.ShapeDtypeStruct((m, n), F32),
        scratch_shapes=[pltpu.VMEM((tm, d), BF16)],
        compiler_params=pltpu.CompilerParams(
            dimension_semantics=("parallel", "arbitrary"), vmem_limit_bytes=VMEM_LIMIT),
        name="inproj",
    )(x, g, w, b)


def _inproj_small_kernel(x_ref, g_ref, w_ref, b_ref, lg_ref, lb_ref, o_ref, ki_ref):
    n = _rms_rows(x_ref[...], g_ref[...])
    o = jnp.dot(n, w_ref[...], preferred_element_type=F32, precision=lax.Precision.HIGHEST) + b_ref[...]
    o_ref[...] = o
    ik = o[:, :IDX_DIM]
    mu = jnp.mean(ik, axis=-1, keepdims=True)
    var = jnp.mean(jnp.square(ik - mu), axis=-1, keepdims=True)
    ki_ref[...] = (ik - mu) * lax.rsqrt(var + EPS) * lg_ref[...] + lb_ref[...]


def _inproj_small(x, g, w, b, ln_g, ln_b, *, tm=512):
    m, d = x.shape
    n = w.shape[1]
    return pl.pallas_call(
        _inproj_small_kernel,
        grid=(m // tm,),
        in_specs=[
            pl.BlockSpec((tm, d), lambda i: (i, 0)),
            pl.BlockSpec((1, d), lambda i: (0, 0)),
            pl.BlockSpec((d, n), lambda i: (0, 0)),
            pl.BlockSpec((1, n), lambda i: (0, 0)),
            pl.BlockSpec((1, IDX_DIM), lambda i: (0, 0)),
            pl.BlockSpec((1, IDX_DIM), lambda i: (0, 0)),
        ],
        out_specs=[pl.BlockSpec((tm, n), lambda i: (i, 0)), pl.BlockSpec((tm, IDX_DIM), lambda i: (i, 0))],
        out_shape=[jax.ShapeDtypeStruct((m, n), F32), jax.ShapeDtypeStruct((m, IDX_DIM), F32)],
        compiler_params=pltpu.CompilerParams(
            dimension_semantics=("parallel",), vmem_limit_bytes=VMEM_LIMIT),
        name="inproj_small",
    )(x, g, w, b, ln_g, ln_b)


def _merge_kernel(x_ref, g_ref, oa_ref, ob_ref, oc_ref, wg0_ref, wg1_ref, wg2_ref,
                  bg0_ref, bg1_ref, bg2_ref, wb0_ref, wb1_ref, wb2_ref, wo_ref, o_ref, n_ref):
    j = pl.program_id(1)

    @pl.when(j == 0)
    def _():
        n_ref[...] = _rms_rows(x_ref[...], g_ref[...]).astype(BF16)
        o_ref[...] = jnp.zeros_like(o_ref)

    n = n_ref[...]
    merged = None
    for o_r, wg_r, bg_r, wb_r in ((oa_ref, wg0_ref, bg0_ref, wb0_ref),
                                  (ob_ref, wg1_ref, bg1_ref, wb1_ref),
                                  (oc_ref, wg2_ref, bg2_ref, wb2_ref)):
        gate = jax.nn.sigmoid(jnp.dot(n, wg_r[...], preferred_element_type=F32) + bg_r[...])
        y = jnp.dot(o_r[...], wb_r[0], preferred_element_type=F32)
        merged = gate * y if merged is None else merged + gate * y
    o_ref[...] += jnp.dot(merged.astype(BF16), wo_ref[...], preferred_element_type=F32)

    @pl.when(j == pl.num_programs(1) - 1)
    def _():
        o_ref[...] = x_ref[...] + o_ref[...]


def _merge(x, g, oa, ob, oc, wg, bg, wb, wo, *, tm=512, tn=512):
    m, d = x.shape
    w = oa.shape[1]
    nj = d // tn
    wg_specs = [pl.BlockSpec((d, tn), functools.partial(lambda i, j, n: (0, n * nj + j), n=n))
                for n in range(N_BRANCH)]
    bg_specs = [pl.BlockSpec((1, tn), functools.partial(lambda i, j, n: (0, n * nj + j), n=n))
                for n in range(N_BRANCH)]
    wb_specs = [pl.BlockSpec((1, w, tn), functools.partial(lambda i, j, n: (n, 0, j), n=n))
                for n in range(N_BRANCH)]
    o_spec = pl.BlockSpec((tm, w), lambda i, j: (i, 0))
    return pl.pallas_call(
        _merge_kernel,
        grid=(m // tm, nj),
        in_specs=[pl.BlockSpec((tm, d), lambda i, j: (i, 0)),
                  pl.BlockSpec((1, d), lambda i, j: (0, 0)),
                  o_spec, o_spec, o_spec,
                  *wg_specs, *bg_specs, *wb_specs,
                  pl.BlockSpec((tn, d), lambda i, j: (j, 0))],
        out_specs=pl.BlockSpec((tm, d), lambda i, j: (i, 0)),
        out_shape=jax.ShapeDtypeStruct((m, d), F32),
        scratch_shapes=[pltpu.VMEM((tm, d), BF16)],
        compiler_params=pltpu.CompilerParams(
            dimension_semantics=("parallel", "arbitrary"), vmem_limit_bytes=VMEM_LIMIT),
        name="merge",
    )(x, g, oa, ob, oc, wg, wg, wg, bg, bg, bg, wb, wb, wb, wo)


def _norm_kernel(x_ref, g_ref, o_ref):
    o_ref[...] = _rms_rows(x_ref[...], g_ref[...])


def _final_norm(x, g, *, tm=512):
    m, d = x.shape
    return pl.pallas_call(
        _norm_kernel,
        grid=(m // tm,),
        in_specs=[pl.BlockSpec((tm, d), lambda i: (i, 0)), pl.BlockSpec((1, d), lambda i: (0, 0))],
        out_specs=pl.BlockSpec((tm, d), lambda i: (i, 0)),
        out_shape=jax.ShapeDtypeStruct((m, d), F32),
        compiler_params=pltpu.CompilerParams(dimension_semantics=("parallel",)),
        name="final_norm",
    )(x, g)


NEG = -1e30
LOG2E = math.log2(math.e)
INT_MIN = -2 ** 31


def _order_key(x):
    bits = lax.bitcast_convert_type(x, jnp.int32)
    return bits ^ ((bits >> 31) & jnp.int32(0x7FFFFFFF))


def _kth_largest_key(count_ge, topk, shape):
    t0 = jnp.where(count_ge(jnp.zeros(shape, jnp.int32)) >= topk, jnp.int32(0), jnp.int32(INT_MIN))

    def bit(b, t):
        cand = t + lax.shift_left(jnp.int32(1), jnp.int32(30) - b)
        return jnp.where(count_ge(cand) >= topk, cand, t)

    return lax.fori_loop(0, 31, bit, t0)


def _kth_largest_key_radix(count_ge, topk, shape, bits):
    t = jnp.where(count_ge(jnp.zeros(shape, jnp.int32)) >= topk, jnp.int32(0), jnp.int32(INT_MIN))
    hi = 31
    while hi > 0:
        nb = hi % bits or bits
        sh = hi - nb
        digit = jnp.zeros(shape, jnp.int32)
        for d in range(1, 2 ** nb):
            digit = digit + jnp.where(count_ge(t + jnp.int32(d << sh)) >= topk, 1, 0)
        t = t + lax.shift_left(digit, jnp.int32(sh))
        hi = sh
    return t


def _softmax_update(s, vb, m_prev, l_prev, acc_prev):
    reps = s.shape[1] // LANE
    m_new = jnp.maximum(m_prev, jnp.max(s, axis=1, keepdims=True))
    alpha = jnp.exp2(m_prev - m_new)
    p = jnp.exp2(s - jnp.concatenate([m_new] * reps, axis=1))
    l_new = alpha * l_prev + jnp.sum(p, axis=1, keepdims=True)
    pv = jnp.dot(p.astype(BF16), vb, preferred_element_type=F32)
    dreps = pv.shape[1] // LANE
    acc_new = jnp.concatenate([alpha] * dreps, axis=1) * acc_prev + pv
    return m_new, l_new, acc_new


def _dsa_prompt_kernel(it_ref, jt_ref, qi_ref, w_ref, ki_ref, q_ref, k_ref, v_ref, bd_ref, bs_ref,
                       o_ref, key_ref, thr_ref, wb_ref, qb_ref, m_ref, l_ref, acc_ref, *, topk, tq, nh):
    i = it_ref[pl.program_id(0)]
    j = jt_ref[pl.program_id(0)]
    rg = 128

    @pl.when(j == 0)
    def _select():
        for h in range(IDX_HEADS):
            wb_ref[h] = jnp.broadcast_to(w_ref[:, h:h + 1], (tq, tq))
        qb_ref[...] = (q_ref[...] * (HEAD_DIM ** -0.5 * LOG2E)).astype(BF16)
        m_ref[...] = jnp.full_like(m_ref, NEG)
        l_ref[...] = jnp.zeros_like(l_ref)
        acc_ref[...] = jnp.zeros_like(acc_ref)
        qi = qi_ref[...].reshape(IDX_HEADS * tq, IDX_DIM)
        row = lax.broadcasted_iota(jnp.int32, (tq, tq), 0)
        col = lax.broadcasted_iota(jnp.int32, (tq, tq), 1)

        def chunk(c, carry):
            kc = ki_ref[pl.ds(pl.multiple_of(c * tq, tq), tq), :]
            s = lax.dot_general(qi, kc, (((1,), (1,)), ((), ())), preferred_element_type=F32)
            sc = jnp.zeros((tq, tq), F32)
            for h in range(IDX_HEADS):
                sc = sc + jnp.maximum(s[h * tq:(h + 1) * tq], 0.0) * wb_ref[h]
            sc = jnp.where(col <= row + jnp.where(c < i, tq, 0), sc, -jnp.inf)
            key_ref[c] = _order_key(sc)
            return carry

        lax.fori_loop(0, i + 1, chunk, 0)

        for r in range(tq // rg):
            rows = pl.ds(r * rg, rg)

            def count_ge(cand):
                def body(c, a):
                    hit = jnp.where(key_ref[c, rows, :] >= cand, 1.0, 0.0)
                    for x in range(tq // LANE):
                        a = a + hit[:, x * LANE:(x + 1) * LANE]
                    return a
                a = lax.fori_loop(0, i + 1, body, jnp.zeros((rg, LANE), F32))
                return jnp.sum(a, axis=1, keepdims=True)

            t = _kth_largest_key(count_ge, float(topk), (rg, 1))
            thr_ref[rows, :] = jnp.broadcast_to(t, (rg, LANE))

    def attend(bias_of_head):
        kk = key_ref[j]
        thr = thr_ref[...]
        selb = jnp.concatenate(
            [jnp.where(kk[:, x * LANE:(x + 1) * LANE] >= thr, 0.0, NEG) for x in range(tq // LANE)], axis=1)
        kb = k_ref[...].astype(BF16)
        vb = v_ref[...].astype(BF16)
        for h in range(nh):
            hs = slice(h * HEAD_DIM, (h + 1) * HEAD_DIM)
            s = lax.dot_general(qb_ref[:, hs], kb[:, hs], (((1,), (1,)), ((), ())),
                                preferred_element_type=F32)
            s = s + selb if bias_of_head is None else s + selb + bias_of_head(h)
            m_new, l_new, acc_new = _softmax_update(s, vb[:, hs], m_ref[h], l_ref[h], acc_ref[:, hs])
            m_ref[h] = m_new
            l_ref[h] = l_new
            acc_ref[:, hs] = acc_new

    @pl.when(j < i - 1)
    def _far():
        attend(None)

    @pl.when(j == i - 1)
    def _sub():
        attend(lambda h: bs_ref[h])

    @pl.when(j == i)
    def _diag():
        attend(lambda h: bd_ref[h])
        for h in range(nh):
            hs = slice(h * HEAD_DIM, (h + 1) * HEAD_DIM)
            o_ref[:, hs] = (acc_ref[:, hs] / l_ref[h]).astype(BF16)


def _t5_bucket(dist):
    n = jnp.maximum(dist, 0)
    max_exact = NUM_BUCKETS // 2
    nf = jnp.maximum(n, 1).astype(F32)
    large = max_exact + (jnp.log(nf / max_exact) / math.log(MAX_DISTANCE / max_exact)
                         * (NUM_BUCKETS - max_exact)).astype(jnp.int32)
    large = jnp.minimum(large, NUM_BUCKETS - 1)
    return jnp.where(n < max_exact, n, large)


def _t5_bias(rel_bias, dist):
    onehot = jax.nn.one_hot(_t5_bucket(dist), NUM_BUCKETS, dtype=F32)
    return jnp.einsum('...b,bh->h...', onehot, rel_bias, precision=lax.Precision.HIGHEST)


def _t5_prompt_tiles(rel_bias, tq):
    assert tq >= MAX_DISTANCE
    r = jnp.arange(tq)[:, None]
    c = jnp.arange(tq)[None, :]
    far = _t5_bias(rel_bias, jnp.int32(tq + 1))[:, None, None]
    bd = jnp.where((r >= c)[None], (_t5_bias(rel_bias, r - c) - far) * LOG2E, NEG)
    bs = (_t5_bias(rel_bias, tq + r - c) - far) * LOG2E
    return bd, bs


def _dsa_prompt(big, row0, t, cols, qi, w, ki, rel_bias, tiles=None, *, tq=DSA_TQ):
    bd, bs = _t5_prompt_tiles(rel_bias, tq) if tiles is None else tiles
    nh = rel_bias.shape[1]
    mw = nh * HEAD_DIM
    assert t % tq == 0 and row0 % tq == 0
    nq = t // tq
    rb = row0 // tq
    topk = min(TOPK_MAX, t // 4)
    pairs = [(i, j) for i in range(nq) for j in range(i + 1)]
    it = jnp.asarray([p[0] for p in pairs], jnp.int32)
    jt = jnp.asarray([p[1] for p in pairs], jnp.int32)
    cq, ck, cv = cols
    kern = functools.partial(_dsa_prompt_kernel, topk=topk, tq=tq, nh=nh)
    grid_spec = pltpu.PrefetchScalarGridSpec(
        num_scalar_prefetch=2,
        grid=(len(pairs),),
        in_specs=[
            pl.BlockSpec((IDX_HEADS, tq, IDX_DIM), lambda s, it, jt: (0, it[s], 0)),
            pl.BlockSpec((tq, IDX_HEADS), lambda s, it, jt: (it[s], 0)),
            pl.BlockSpec((t, IDX_DIM), lambda s, it, jt: (0, 0)),
            pl.BlockSpec((tq, mw), lambda s, it, jt: (rb + it[s], cq)),
            pl.BlockSpec((tq, mw), lambda s, it, jt: (rb + jt[s], ck)),
            pl.BlockSpec((tq, mw), lambda s, it, jt: (rb + jt[s], cv)),
            pl.BlockSpec((nh, tq, tq), lambda s, it, jt: (0, 0, 0)),
            pl.BlockSpec((nh, tq, tq), lambda s, it, jt: (0, 0, 0)),
        ],
        out_specs=pl.BlockSpec((tq, mw), lambda s, it, jt: (it[s], 0)),
        scratch_shapes=[
            pltpu.VMEM((nq, tq, tq), jnp.int32),
            pltpu.VMEM((tq, LANE), jnp.int32),
            pltpu.VMEM((IDX_HEADS, tq, tq), F32),
            pltpu.VMEM((tq, mw), BF16),
            pltpu.VMEM((nh, tq, LANE), F32),
            pltpu.VMEM((nh, tq, LANE), F32),
            pltpu.VMEM((tq, mw), F32),
        ],
    )
    return pl.pallas_call(
        kern,
        grid_spec=grid_spec,
        out_shape=jax.ShapeDtypeStruct((t, mw), BF16),
        compiler_params=pltpu.CompilerParams(
            dimension_semantics=("arbitrary",), vmem_limit_bytes=VMEM_LIMIT),
        name="dsa_prompt",
    )(it, jt, qi, w, ki, big, big, big, bd, bs)


def _dsa_sample_kernel(pt_ref, qi_ref, w_ref, kin_ref, q_ref, kn_ref, vn_ref, bias_ref, biasn_ref, *rest,
                       topk, t, n_pages, g_pages, nh, past):
    idx_refs = rest[:n_pages]
    kp_refs = rest[n_pages:n_pages + g_pages]
    vp_refs = rest[n_pages + g_pages:n_pages + 2 * g_pages]
    o_ref, selb_ref, selbn_ref, qbd_ref, m_ref, l_ref, acc_ref = rest[n_pages + 2 * g_pages:]
    g = pl.program_id(1)
    ng = n_pages // g_pages
    gw = g_pages * PAGE_SIZE
    rows = nh * t
    mw = nh * HEAD_DIM

    @pl.when(g == 0)
    def _select():
        qi = qi_ref[...]
        wcol = w_ref[...]
        pad = jnp.zeros((PAGE_SIZE - t, IDX_DIM), F32)
        keys = jnp.concatenate([r[...] for r in idx_refs] + [kin_ref[...], pad], axis=0).astype(BF16)
        s = lax.dot_general(qi, keys, (((1,), (1,)), ((), ())), preferred_element_type=F32)
        s = jnp.maximum(s, 0.0) * wcol
        sc = s[0:t]
        for h in range(1, IDX_HEADS):
            sc = sc + s[h * t:(h + 1) * t]
        kpos = lax.broadcasted_iota(jnp.int32, sc.shape, 1)
        qrow = lax.broadcasted_iota(jnp.int32, sc.shape, 0)
        key = _order_key(jnp.where(kpos <= past + qrow, sc, -jnp.inf))

        def count_ge(cand):
            return jnp.sum(jnp.where(key >= cand, 1.0, 0.0), axis=1, keepdims=True)

        thr = _kth_largest_key_radix(count_ge, float(topk), (t, 1), 4)
        selb = jnp.where(key >= thr, 0.0, NEG)
        for x in range(ng):
            selb_ref[x] = selb[:, x * gw:(x + 1) * gw]
        selbn_ref[...] = selb[:, past:]

        qs = jnp.concatenate([q_ref[...] * (HEAD_DIM ** -0.5 * LOG2E)] * nh, axis=0)
        rr = lax.broadcasted_iota(jnp.int32, (rows, mw), 0) >> (t.bit_length() - 1)
        cc = lax.broadcasted_iota(jnp.int32, (rows, mw), 1) >> (HEAD_DIM.bit_length() - 1)
        qbd_ref[...] = jnp.where(rr == cc, qs, 0.0).astype(BF16)
        m_ref[...] = jnp.full_like(m_ref, NEG)
        l_ref[...] = jnp.zeros_like(l_ref)
        acc_ref[...] = jnp.zeros_like(acc_ref)

    def update(kb, vb, bias):
        s = lax.dot_general(qbd_ref[...], kb, (((1,), (1,)), ((), ())), preferred_element_type=F32)
        m_new, l_new, acc_new = _softmax_update(s + bias, vb, m_ref[...], l_ref[...], acc_ref[...])
        m_ref[...] = m_new
        l_ref[...] = l_new
        acc_ref[...] = acc_new

    def page(r):
        return jnp.concatenate([r[pl.ds(h, PAGE_SIZE, stride=nh), :] for h in range(nh)], axis=1)

    kb = jnp.concatenate([page(r) for r in kp_refs], axis=0).astype(BF16)
    vb = jnp.concatenate([page(r) for r in vp_refs], axis=0).astype(BF16)
    update(kb, vb, bias_ref[g] + jnp.concatenate([selb_ref[g]] * nh, axis=0))

    @pl.when(g == ng - 1)
    def _finish():
        padn = jnp.zeros((PAGE_SIZE - t, mw), F32)
        kn = jnp.concatenate([kn_ref[...], padn], axis=0).astype(BF16)
        vn = jnp.concatenate([vn_ref[...], padn], axis=0).astype(BF16)
        update(kn, vn, biasn_ref[...] + jnp.concatenate([selbn_ref[...]] * nh, axis=0))
        for h in range(nh):
            hs = slice(h * HEAD_DIM, (h + 1) * HEAD_DIM)
            rs = slice(h * t, (h + 1) * t)
            o_ref[:, hs] = (acc_ref[rs, hs] / l_ref[rs, :]).astype(BF16)


def _dsa_sample(big, row0, b, t, cols, qi, w, kin, cache_k, cache_v, cache_idx_k, l, page_table, rel_bias):
    nh = rel_bias.shape[1]
    mw = nh * HEAD_DIM
    n_pages = page_table.shape[1]
    past = n_pages * PAGE_SIZE
    g_pages = math.gcd(n_pages, 4)
    ng = n_pages // g_pages
    gw = g_pages * PAGE_SIZE
    topk = min(TOPK_MAX, (past + t) // 4)
    assert row0 % t == 0 and t % 8 == 0 and t <= PAGE_SIZE and t & (t - 1) == 0
    rb = row0 // t
    rows = nh * t
    qpos = past + jnp.arange(t)
    kpos = jnp.arange(past + PAGE_SIZE)
    dist = qpos[:, None] - kpos[None, :]
    bias = jnp.where((dist >= 0)[None], _t5_bias(rel_bias, dist) * LOG2E, NEG)
    bias = bias.reshape(rows, past + PAGE_SIZE)
    bias_past = bias[:, :past].reshape(rows, ng, gw).transpose(1, 0, 2)
    bias_new = bias[:, past:]
    cq, ck, cv = cols

    def page_map(x, bi, gi, pt):
        return (l, pt[bi, gi * g_pages + x], 0, 0)

    def idx_map(x, bi, gi, pt):
        return (l, pt[bi, x], 0, 0)

    in_specs = [
        pl.BlockSpec((None, IDX_HEADS * t, IDX_DIM), lambda bi, gi, pt: (bi, 0, 0)),
        pl.BlockSpec((None, IDX_HEADS * t, 1), lambda bi, gi, pt: (bi, 0, 0)),
        pl.BlockSpec((None, t, IDX_DIM), lambda bi, gi, pt: (bi, 0, 0)),
        pl.BlockSpec((t, mw), lambda bi, gi, pt: (rb + bi, cq)),
        pl.BlockSpec((t, mw), lambda bi, gi, pt: (rb + bi, ck)),
        pl.BlockSpec((t, mw), lambda bi, gi, pt: (rb + bi, cv)),
        pl.BlockSpec((ng, rows, gw), lambda bi, gi, pt: (0, 0, 0)),
        pl.BlockSpec((rows, PAGE_SIZE), lambda bi, gi, pt: (0, 0)),
    ]
    in_specs += [pl.BlockSpec((None, None, PAGE_SIZE, IDX_DIM), functools.partial(idx_map, x))
                 for x in range(n_pages)]
    in_specs += [pl.BlockSpec((None, None, PAGE_SIZE * nh, HEAD_DIM), functools.partial(page_map, x % g_pages))
                 for x in range(2 * g_pages)]
    kern = functools.partial(_dsa_sample_kernel, topk=topk, t=t, n_pages=n_pages, g_pages=g_pages,
                             nh=nh, past=past)
    grid_spec = pltpu.PrefetchScalarGridSpec(
        num_scalar_prefetch=1,
        grid=(b, ng),
        in_specs=in_specs,
        out_specs=pl.BlockSpec((t, mw), lambda bi, gi, pt: (bi, 0)),
        scratch_shapes=[
            pltpu.VMEM((ng, t, gw), F32),
            pltpu.VMEM((t, PAGE_SIZE), F32),
            pltpu.VMEM((rows, mw), BF16),
            pltpu.VMEM((rows, LANE), F32),
            pltpu.VMEM((rows, LANE), F32),
            pltpu.VMEM((rows, mw), F32),
        ],
    )
    return pl.pallas_call(
        kern,
        grid_spec=grid_spec,
        out_shape=jax.ShapeDtypeStruct((b * t, mw), BF16),
        compiler_params=pltpu.CompilerParams(
            dimension_semantics=("parallel", "arbitrary"), vmem_limit_bytes=VMEM_LIMIT),
        name="dsa_sample",
    )(page_table, qi, w, kin, big, big, big, bias_past, bias_new,
      *([cache_idx_k] * n_pages), *([cache_k] * g_pages), *([cache_v] * g_pages))


def _lane_expand(x, nh):
    c = x.shape[0]
    return jnp.concatenate([jnp.broadcast_to(x[:, h:h + 1], (c, HEAD_DIM)) for h in range(nh)], axis=1)


def _bdot(a, b):
    return jnp.dot(a.astype(BF16), b.astype(BF16), preferred_element_type=F32)


def _dot01(a01, b):
    a = a01.astype(BF16)
    out = None
    rest = b
    for _ in range(3):
        part = rest.astype(BF16)
        rest = rest - part.astype(F32)
        term = jnp.dot(a, part, preferred_element_type=F32)
        out = term if out is None else out + term
    return out


def _bdot_nt(a, b):
    return lax.dot_general(a.astype(BF16), b.astype(BF16), (((1,), (1,)), ((), ())), preferred_element_type=F32)


def _bdot_tn(a, b):
    return lax.dot_general(a.astype(BF16), b.astype(BF16), (((0,), (0,)), ((), ())), preferred_element_type=F32)


def _softplus(x):
    return jnp.maximum(x, 0.0) + jnp.log(1.0 + jnp.exp(-jnp.abs(x)))


def _gdn_kernel(qkv_ref, z_ref, sm_ref, cw_ref, alog_ref, dtb_ref, ng_ref, s0_ref, conv0_ref,
                o_ref, s_out_ref, conv_out_ref, s_ref, xp_ref, *, c, nh, ob, oa):
    n = pl.program_id(1)
    mw = nh * HEAD_DIM
    halo = 8

    @pl.when(n == 0)
    def _():
        s_ref[...] = s0_ref[...]
        xp_ref[0:halo, :] = conv0_ref[...]

    xp_ref[halo:halo + c, :] = qkv_ref[...]
    y = cw_ref[0:1, :] * xp_ref[halo - CONV_W + 1:halo - CONV_W + 1 + c, :]
    for j in range(1, CONV_W):
        y = y + cw_ref[j:j + 1, :] * xp_ref[halo - CONV_W + 1 + j:halo - CONV_W + 1 + j + c, :]
    act = y * jax.nn.sigmoid(y)

    sm = sm_ref[...]
    beta = jax.nn.sigmoid(sm[:, ob:ob + nh])
    g = -jnp.exp(alog_ref[...]) * _softplus(sm[:, oa:oa + nh] + dtb_ref[...])
    beta_e = _lane_expand(beta, nh)
    g_e = _lane_expand(g, nh)
    t_i = lax.broadcasted_iota(jnp.int32, (c, mw), 0)
    j_i = lax.broadcasted_iota(jnp.int32, (c, mw), 1) & (HEAD_DIM - 1)
    ii = lax.broadcasted_iota(jnp.int32, (c, c), 0)
    jj = lax.broadcasted_iota(jnp.int32, (c, c), 1)
    tri = jnp.where(jj <= ii, 1.0, 0.0)
    cum = _dot01(tri, jnp.concatenate([jnp.where(t_i > j_i, g_e, 0.0), g, jnp.zeros((c, LANE - nh), F32)], axis=1))
    dmat = cum[:, :mw]
    gc = _lane_expand(cum[:, mw:mw + nh], nh)
    g_end = gc[c - 1:c, :]
    e_gc = jnp.exp(gc)
    e_rest = jnp.exp(g_end - gc)
    e_end = jnp.exp(g_end)
    eye = jnp.where(ii == jj, 1.0, 0.0)

    heads = range(nh)
    hsl = [slice(h * HEAD_DIM, (h + 1) * HEAD_DIM) for h in heads]
    qn, kn, kb, dec = [], [], [], []
    for h in heads:
        qh = act[:, h * HEAD_DIM:(h + 1) * HEAD_DIM]
        kh = act[:, mw + h * HEAD_DIM:mw + (h + 1) * HEAD_DIM]
        qn.append(qh * lax.rsqrt(jnp.sum(qh * qh, axis=1, keepdims=True) + EPS) * HEAD_DIM ** -0.5)
        kn.append(kh * lax.rsqrt(jnp.sum(kh * kh, axis=1, keepdims=True) + EPS))
        kb.append(kn[h] * beta_e[:, hsl[h]])
        dec.append(jnp.where(ii >= jj, jnp.exp(dmat[:, h * HEAD_DIM:h * HEAD_DIM + c]), 0.0))
    knb = [kn[h].astype(BF16) for h in heads]
    kk = [_bdot_nt(kb[h], knb[h]) for h in heads]
    qk = [_bdot_nt(qn[h], knb[h]) for h in heads]
    x = [jnp.where(ii > jj, -kk[h] * dec[h], 0.0) for h in heads]
    p = [eye + x[h] for h in heads]
    for _ in range(c.bit_length() - 2):
        x = [_bdot(x[h], x[h]) for h in heads]
        p = [p[h] + _bdot(p[h], x[h]) for h in heads]
    uw = [_bdot(p[h], jnp.concatenate(
        [act[:, 2 * mw + h * HEAD_DIM:2 * mw + (h + 1) * HEAD_DIM] * beta_e[:, hsl[h]],
         kb[h] * e_gc[:, hsl[h]]], axis=1)) for h in heads]
    s_old = [s_ref[h] for h in heads]
    sb = [s_old[h].astype(BF16) for h in heads]
    o_inter = [_bdot(qn[h] * e_gc[:, hsl[h]], sb[h]) for h in heads]
    v_new = [uw[h][:, :HEAD_DIM] - _bdot(uw[h][:, HEAD_DIM:], sb[h]) for h in heads]
    o_intra = [_bdot(qk[h] * dec[h], v_new[h]) for h in heads]
    kv = [_bdot_tn(kn[h] * e_rest[:, hsl[h]], v_new[h]) for h in heads]
    for h in heads:
        s_ref[h] = s_old[h] * e_end[:, hsl[h]] + kv[h]
        o = o_inter[h] + o_intra[h]
        on = o * lax.rsqrt(jnp.mean(o * o, axis=1, keepdims=True) + EPS) * ng_ref[...]
        zh = z_ref[:, hsl[h]]
        o_ref[:, hsl[h]] = (on * zh * jax.nn.sigmoid(zh)).astype(o_ref.dtype)

    tail = xp_ref[c:c + halo, :]
    xp_ref[0:halo, :] = tail

    @pl.when(n == pl.num_programs(1) - 1)
    def _():
        s_out_ref[...] = s_ref[...]
        conv_out_ref[...] = tail


def _gdn(big, small, row0, b, t, cq, cz, ob, oa, conv_w, a_log, dt_bias, norm_g, s0, conv0):
    nh = s0.shape[1]
    mw = nh * HEAD_DIM
    c = math.gcd(t, CHUNK)
    assert c % 8 == 0 and c & (c - 1) == 0 and row0 % c == 0
    nc = t // c
    rb = row0 // c
    halo = 8
    conv0p = jnp.pad(conv0, ((0, 0), (halo - (CONV_W - 1), 0), (0, 0)))
    odt = BF16 if c % 16 == 0 else F32
    kern = functools.partial(_gdn_kernel, c=c, nh=nh, ob=ob, oa=oa)
    o, s_new, conv_new = pl.pallas_call(
        kern,
        grid=(b, nc),
        in_specs=[
            pl.BlockSpec((c, 3 * mw), lambda bi, n: (rb + bi * nc + n, cq)),
            pl.BlockSpec((c, mw), lambda bi, n: (rb + bi * nc + n, cz)),
            pl.BlockSpec((c, LANE), lambda bi, n: (rb + bi * nc + n, 0)),
            pl.BlockSpec((CONV_W, 3 * mw), lambda bi, n: (0, 0)),
            pl.BlockSpec((1, nh), lambda bi, n: (0, 0)),
            pl.BlockSpec((1, nh), lambda bi, n: (0, 0)),
            pl.BlockSpec((1, HEAD_DIM), lambda bi, n: (0, 0)),
            pl.BlockSpec((None, nh, HEAD_DIM, HEAD_DIM), lambda bi, n: (bi, 0, 0, 0)),
            pl.BlockSpec((None, halo, 3 * mw), lambda bi, n: (bi, 0, 0)),
        ],
        out_specs=[
            pl.BlockSpec((c, mw), lambda bi, n: (bi * nc + n, 0)),
            pl.BlockSpec((None, nh, HEAD_DIM, HEAD_DIM), lambda bi, n: (bi, 0, 0, 0)),
            pl.BlockSpec((None, halo, 3 * mw), lambda bi, n: (bi, 0, 0)),
        ],
        out_shape=[
            jax.ShapeDtypeStruct((b * t, mw), odt),
            jax.ShapeDtypeStruct((b, nh, HEAD_DIM, HEAD_DIM), F32),
            jax.ShapeDtypeStruct((b, halo, 3 * mw), F32),
        ],
        scratch_shapes=[pltpu.VMEM((nh, HEAD_DIM, HEAD_DIM), F32), pltpu.VMEM((halo + c, 3 * mw), F32)],
        compiler_params=pltpu.CompilerParams(
            dimension_semantics=("parallel", "arbitrary"), vmem_limit_bytes=VMEM_LIMIT),
        name="gdn",
    )(big, big, small, conv_w, a_log[None], dt_bias[None], norm_g[None], s0, conv0p)
    return o, s_new, conv_new[:, halo - (CONV_W - 1):]


def _mlstm_kernel(q_ref, k_ref, v_ref, og_ref, sm_ref, ng_ref, c0_ref, n0_ref, m0_ref,
                  o_ref, c_out_ref, n_out_ref, m_out_ref, c_ref, n_ref, m_ref, *, c, nh, oi, of):
    n = pl.program_id(1)
    mw = nh * HEAD_DIM

    @pl.when(n == 0)
    def _():
        c_ref[...] = c0_ref[...]
        n_ref[...] = n0_ref[...]
        m_ref[...] = m0_ref[...]

    sm = sm_ref[...]
    ig_e = _lane_expand(sm[:, oi:oi + nh], nh)
    lf_e = _lane_expand(-_softplus(-sm[:, of:of + nh]), nh)
    t_i = lax.broadcasted_iota(jnp.int32, (c, mw), 0)
    j_i = lax.broadcasted_iota(jnp.int32, (c, mw), 1) & (HEAD_DIM - 1)
    ii = lax.broadcasted_iota(jnp.int32, (c, c), 0)
    jj = lax.broadcasted_iota(jnp.int32, (c, c), 1)
    lhs = jnp.concatenate([jnp.where(jj <= ii, 1.0, 0.0), jnp.ones((c, c), F32)], axis=1)
    rhs = jnp.concatenate([
        jnp.concatenate([lf_e, jnp.where(t_i > j_i, lf_e, 0.0)], axis=1),
        jnp.concatenate([jnp.zeros((c, mw), F32), jnp.where(t_i == j_i, ig_e, 0.0)], axis=1)], axis=0)
    cum = jnp.dot(lhs, rhs, preferred_element_type=F32, precision=lax.Precision.HIGHEST)
    bc = cum[:, :mw]
    dlog = cum[:, mw:]
    b_end = bc[c - 1:c, :]
    a_end = b_end - bc + ig_e

    heads = range(nh)
    hsl = [slice(h * HEAD_DIM, (h + 1) * HEAD_DIM) for h in heads]
    qb = [(q_ref[:, hsl[h]] * HEAD_DIM ** -0.5).astype(BF16) for h in heads]
    kh = [k_ref[:, hsl[h]] for h in heads]
    vb = [v_ref[:, hsl[h]].astype(BF16) for h in heads]
    cs = [c_ref[h] for h in heads]
    qk = [_bdot_nt(qb[h], kh[h]) for h in heads]
    qc = [_bdot(qb[h], cs[h]) for h in heads]
    m_prev = [m_ref[h:h + 1, :] for h in heads]
    m_new = [jnp.maximum(b_end[:, hsl[h]] + m_prev[h], jnp.max(a_end[:, hsl[h]], axis=0, keepdims=True))
             for h in heads]
    kw = [kh[h] * jnp.exp(a_end[:, hsl[h]] - m_new[h]) for h in heads]
    kv = [_bdot_tn(kw[h], vb[h]) for h in heads]
    mt, sc = [], []
    for h in heads:
        dl = jnp.where(ii >= jj, dlog[:, h * HEAD_DIM:h * HEAD_DIM + c], -jnp.inf)
        inter = bc[:, hsl[h]] + m_prev[h]
        mt.append(jnp.maximum(inter, jnp.max(dl, axis=1, keepdims=True)))
        sc.append(qk[h] * jnp.exp(dl - mt[h][:, :c]))
    sv = [_bdot(sc[h], vb[h]) for h in heads]
    for h in heads:
        w_inter = jnp.exp(bc[:, hsl[h]] + m_prev[h] - mt[h])
        ns = n_ref[h:h + 1, :]
        qf = q_ref[:, hsl[h]] * HEAD_DIM ** -0.5
        num = w_inter * qc[h] + sv[h]
        den = w_inter * jnp.sum(qf * ns, axis=1, keepdims=True) + jnp.sum(sc[h], axis=1, keepdims=True)
        hh = num / jnp.maximum(jnp.abs(den), jnp.exp(-mt[h]))
        keep = jnp.exp(b_end[:, hsl[h]] + m_prev[h] - m_new[h])
        c_ref[h] = keep * cs[h] + kv[h]
        n_ref[h:h + 1, :] = keep * ns + jnp.sum(kw[h], axis=0, keepdims=True)
        m_ref[h:h + 1, :] = m_new[h]
        hn = hh * lax.rsqrt(jnp.mean(hh * hh, axis=1, keepdims=True) + EPS) * ng_ref[...]
        o_ref[:, hsl[h]] = (hn * jax.nn.sigmoid(og_ref[:, hsl[h]])).astype(o_ref.dtype)

    @pl.when(n == pl.num_programs(1) - 1)
    def _():
        c_out_ref[...] = c_ref[...]
        n_out_ref[...] = n_ref[...]
        m_out_ref[...] = m_ref[...]


def _mlstm(big, small, row0, b, t, cols, oi, of, norm_g, c0, n0, m0):
    nh = c0.shape[1]
    mw = nh * HEAD_DIM
    c = math.gcd(t, CHUNK)
    assert c % 8 == 0 and c <= HEAD_DIM and row0 % c == 0
    nc = t // c
    rb = row0 // c
    odt = BF16 if c % 16 == 0 else F32
    m0e = jnp.broadcast_to(m0[:, :, None], (b, nh, LANE))

    def col_spec(cc):
        return pl.BlockSpec((c, mw), lambda bi, n: (rb + bi * nc + n, cc))

    kern = functools.partial(_mlstm_kernel, c=c, nh=nh, oi=oi, of=of)
    o, c_new, n_new, m_new = pl.pallas_call(
        kern,
        grid=(b, nc),
        in_specs=[
            *[col_spec(cc) for cc in cols],
            pl.BlockSpec((c, LANE), lambda bi, n: (rb + bi * nc + n, 0)),
            pl.BlockSpec((1, HEAD_DIM), lambda bi, n: (0, 0)),
            pl.BlockSpec((None, nh, HEAD_DIM, HEAD_DIM), lambda bi, n: (bi, 0, 0, 0)),
            pl.BlockSpec((None, nh, HEAD_DIM), lambda bi, n: (bi, 0, 0)),
            pl.BlockSpec((None, nh, LANE), lambda bi, n: (bi, 0, 0)),
        ],
        out_specs=[
            pl.BlockSpec((c, mw), lambda bi, n: (bi * nc + n, 0)),
            pl.BlockSpec((None, nh, HEAD_DIM, HEAD_DIM), lambda bi, n: (bi, 0, 0, 0)),
            pl.BlockSpec((None, nh, HEAD_DIM), lambda bi, n: (bi, 0, 0)),
            pl.BlockSpec((None, nh, LANE), lambda bi, n: (bi, 0, 0)),
        ],
        out_shape=[
            jax.ShapeDtypeStruct((b * t, mw), odt),
            jax.ShapeDtypeStruct((b, nh, HEAD_DIM, HEAD_DIM), F32),
            jax.ShapeDtypeStruct((b, nh, HEAD_DIM), F32),
            jax.ShapeDtypeStruct((b, nh, LANE), F32),
        ],
        scratch_shapes=[pltpu.VMEM((nh, HEAD_DIM, HEAD_DIM), F32), pltpu.VMEM((nh, HEAD_DIM), F32),
                        pltpu.VMEM((nh, LANE), F32)],
        compiler_params=pltpu.CompilerParams(
            dimension_semantics=("parallel", "arbitrary"), vmem_limit_bytes=VMEM_LIMIT),
        name="mlstm",
    )(big, big, big, big, small, norm_g[None], c0, n0, m0e)
    return o, c_new, n_new, m_new[:, :, 0]


def _segments(d_model):
    mw = d_model // 2
    nh = mw // HEAD_DIM
    widths = (3 * mw, mw, nh, nh, mw, mw, mw, mw, nh, nh, mw, mw, mw,
              IDX_HEADS * IDX_DIM, IDX_HEADS, IDX_DIM, N_BRANCH * d_model)
    offs = [0]
    for w in widths:
        offs.append(offs[-1] + w)
    names = ('g_qkv', 'g_z', 'g_b', 'g_a', 'm_q', 'm_k', 'm_v', 'm_o', 'm_i', 'm_f',
             'a_q', 'a_k', 'a_v', 'i_q', 'i_w', 'i_k', 'gates')
    return {nm: (offs[i], widths[i]) for i, nm in enumerate(names)}


_BIG = ('g_qkv', 'g_z', 'm_q', 'm_k', 'm_v', 'm_o', 'a_q', 'a_k', 'a_v', 'i_q')
_SMALL = ('i_k', 'g_b', 'g_a', 'm_i', 'm_f', 'i_w')


def _take_cols(a, seg, names):
    return jnp.concatenate([a[..., seg[n][0]:seg[n][0] + seg[n][1]] for n in names], axis=-1)


def _layout(seg, names):
    out, off = {}, 0
    for n in names:
        out[n] = (off, seg[n][1])
        off += seg[n][1]
    return out, off


def _trunk_layer(xs, l, P, rel_bias, sample_ctx, n_prompt, prompt_shape, sample_shape, init_sample):
    d = xs.shape[1]
    mw = d // 2
    nh = mw // HEAD_DIM
    bp, tp = prompt_shape
    bs, ts = sample_shape

    x1 = _ffn(xs, P['norm_g'][l, 0][None], P['ffa_w13'][l], P['ffa_w2'][l])
    g1 = P['norm_g'][l, 1][None]
    big = _inproj(x1, g1, P['w_big'][l], P['b_big'][l][None])
    small, kidx = _inproj_small(x1, g1, P['w_small'][l], P['b_small'][l][None],
                                P['idx_ln_g'][l][None], P['idx_ln_b'][l][None])
    lb, ls = P['lay_big'], P['lay_small']
    att_cols = tuple(lb[n][0] // mw for n in ('a_q', 'a_k', 'a_v'))
    idx_scale = IDX_DIM ** -0.5 * IDX_HEADS ** -0.5

    outs = []
    states = []
    for grp in range(2):
        if grp == 0:
            b, t = bp, tp
            init = (jnp.zeros((bp, nh, HEAD_DIM, HEAD_DIM), F32), jnp.zeros((bp, CONV_W - 1, 3 * mw), F32),
                    jnp.zeros((bp, nh, HEAD_DIM, HEAD_DIM), F32), jnp.zeros((bp, nh, HEAD_DIM), F32),
                    jnp.zeros((bp, nh), F32))
        else:
            b, t = bs, ts
            init = tuple(a[l] for a in init_sample)
        s0, conv0, c0, n0, m0 = init
        row0 = 0 if grp == 0 else n_prompt

        def seg_b(name):
            o, w = lb[name]
            return lax.slice(big, (row0, o), (row0 + b * t, o + w))

        def seg_s(name):
            o, w = ls[name]
            return lax.slice(small, (row0, o), (row0 + b * t, o + w))

        o_a, s_new, conv_new = _gdn(
            big, small, row0, b, t, lb['g_qkv'][0] // (3 * mw), lb['g_z'][0] // mw, ls['g_b'][0], ls['g_a'][0],
            P['gdn_conv_w'][l], P['gdn_a_log'][l], P['gdn_dt_bias'][l], P['gdn_norm_g'][l], s0, conv0)
        o_b, c_new, n_new, m_new = _mlstm(
            big, small, row0, b, t, tuple(lb[nm][0] // mw for nm in ('m_q', 'm_k', 'm_v', 'm_o')),
            ls['m_i'][0], ls['m_f'][0], P['mlstm_norm_g'][l], c0, n0, m0)
        k_att = seg_b('a_k').reshape(b, t, nh, HEAD_DIM)
        v_att = seg_b('a_v').reshape(b, t, nh, HEAD_DIM)
        k_idx = kidx[row0:row0 + b * t].reshape(b, t, IDX_DIM)
        q_idx = seg_b('i_q').reshape(b, t, IDX_HEADS, IDX_DIM).transpose(0, 2, 1, 3).astype(BF16)
        w_idx = seg_s('i_w').reshape(b, t, IDX_HEADS) * idx_scale
        if grp == 0:
            o_c = jnp.concatenate([
                _dsa_prompt(big, row0 + bi * t, t, att_cols, q_idx[bi], w_idx[bi], k_idx[bi].astype(BF16),
                            rel_bias, P['t5_tiles']) for bi in range(b)], axis=0)
        else:
            cache_k, cache_v, cache_idx_k, page_table = sample_ctx
            o_c = _dsa_sample(big, row0, b, t, att_cols, q_idx.reshape(b, IDX_HEADS * t, IDX_DIM),
                              w_idx.transpose(0, 2, 1).reshape(b, IDX_HEADS * t, 1), k_idx,
                              cache_k.reshape(*cache_k.shape[:2], PAGE_SIZE * nh, HEAD_DIM),
                              cache_v.reshape(*cache_v.shape[:2], PAGE_SIZE * nh, HEAD_DIM),
                              cache_idx_k, l, page_table, rel_bias)
        outs.append((o_a.reshape(b * t, mw).astype(BF16), o_b.reshape(b * t, mw).astype(BF16), o_c))
        states.append((k_att, v_att, k_idx, s_new, conv_new, c_new, n_new, m_new))

    oa = jnp.concatenate([outs[0][0], outs[1][0]], axis=0)
    ob = jnp.concatenate([outs[0][1], outs[1][1]], axis=0)
    oc = jnp.concatenate([outs[0][2], outs[1][2]], axis=0)
    x2 = _merge(x1, g1, oa, ob, oc, P['w_gates'][l], P['b_gates'][l][None], P['w_branch'][l], P['w_out'][l])
    x3 = _ffn(x2, P['norm_g'][l, 2][None], P['ffb_w13'][l], P['ffb_w2'][l])
    return x3, states


def _repack_kernel(x_ref, o_ref, *, nb_valid, nb_padded):
    jb = pl.program_id(1) % nb_padded
    o_ref[...] = jnp.where(jb < nb_valid, x_ref[...], 0.0).astype(o_ref.dtype)


def _repack_bf16(w, axis, f, fp):
    nl, a, b = w.shape
    groups = w.shape[axis] // f
    assert f % LANE == 0 and fp % LANE == 0 and groups * f == w.shape[axis]
    nbv, nbp = f // LANE, fp // LANE

    def src_block(j):
        return (j // nbp) * nbv + jnp.minimum(j % nbp, nbv - 1)

    if axis == 2:
        in_spec = pl.BlockSpec((None, a, LANE), lambda l, j: (l, 0, src_block(j)))
        out_spec = pl.BlockSpec((None, a, LANE), lambda l, j: (l, 0, j))
        out_shape = (nl, a, groups * fp)
    else:
        in_spec = pl.BlockSpec((None, LANE, b), lambda l, j: (l, src_block(j), 0))
        out_spec = pl.BlockSpec((None, LANE, b), lambda l, j: (l, j, 0))
        out_shape = (nl, groups * fp, b)
    return pl.pallas_call(
        functools.partial(_repack_kernel, nb_valid=nbv, nb_padded=nbp),
        grid=(nl, groups * nbp),
        in_specs=[in_spec],
        out_specs=out_spec,
        out_shape=jax.ShapeDtypeStruct(out_shape, BF16),
        compiler_params=pltpu.CompilerParams(dimension_semantics=("parallel", "arbitrary")),
        name="repack",
    )(w)


def _prep_ffn(w13, w2, tf=512):
    f = w2.shape[1]
    fp = _round_up(f, tf)
    return _repack_bf16(w13, 2, f, fp), _repack_bf16(w2, 1, f, fp)


def kernel(x_prompt, x_sample, cache_k, cache_v, cache_idx_k, state_gdn, state_gdn_conv, state_mlstm_c,
           state_mlstm_n, state_mlstm_m, page_table, norm_g, final_g, ffa_w13, ffa_w2, ffb_w13, ffb_w2,
           w_in, b_in, gdn_conv_w, gdn_a_log, gdn_dt_bias, gdn_norm_g, mlstm_norm_g, idx_ln_g, idx_ln_b,
           rel_bias, w_branch, w_out):
    bp, tp, d = x_prompt.shape
    bs, ts, _ = x_sample.shape
    depth = w_in.shape[0]
    seg = _segments(d)
    lay_big, _ = _layout(seg, _BIG)
    lay_small, n_small = _layout(seg, _SMALL)
    pad_small = _round_up(n_small, LANE) - n_small

    ffa_w13p, ffa_w2p = _prep_ffn(ffa_w13, ffa_w2)
    ffb_w13p, ffb_w2p = _prep_ffn(ffb_w13, ffb_w2)
    go, gw = seg['gates']
    w_in_b = lax.optimization_barrier(w_in.astype(BF16))
    P = dict(
        norm_g=norm_g,
        ffa_w13=ffa_w13p, ffa_w2=ffa_w2p,
        ffb_w13=ffb_w13p, ffb_w2=ffb_w2p,
        w_big=_take_cols(w_in_b, seg, _BIG), b_big=_take_cols(b_in, seg, _BIG),
        w_small=jnp.pad(_take_cols(w_in, seg, _SMALL), ((0, 0), (0, 0), (0, pad_small))),
        b_small=jnp.pad(_take_cols(b_in, seg, _SMALL), ((0, 0), (0, pad_small))),
        w_gates=w_in_b[:, :, go:go + gw], b_gates=b_in[:, go:go + gw],
        w_branch=w_branch.astype(BF16), w_out=w_out.astype(BF16),
        gdn_conv_w=gdn_conv_w, gdn_a_log=gdn_a_log, gdn_dt_bias=gdn_dt_bias, gdn_norm_g=gdn_norm_g,
        mlstm_norm_g=mlstm_norm_g, idx_ln_g=idx_ln_g, idx_ln_b=idx_ln_b,
        lay_big=lay_big, lay_small=lay_small,
        t5_tiles=_t5_prompt_tiles(rel_bias, DSA_TQ),
    )

    n_prompt = bp * tp
    xs = jnp.concatenate([x_prompt.reshape(n_prompt, d), x_sample.reshape(bs * ts, d)], axis=0)
    init_sample = (state_gdn, state_gdn_conv, state_mlstm_c, state_mlstm_n, state_mlstm_m)
    sample_ctx = (cache_k, cache_v, cache_idx_k, page_table)
    p_st, s_st = [], []
    for l in range(depth):
        xs, (p, s) = _trunk_layer(xs, l, P, rel_bias, sample_ctx, n_prompt, (bp, tp), (bs, ts), init_sample)
        p_st.append(p)
        s_st.append(s)
    y = _final_norm(xs, final_g[None])
    y_prompt = y[:n_prompt].reshape(bp, tp, d)
    y_sample = y[n_prompt:].reshape(bs, ts, d)

    def stacked(states, i):
        return jnp.stack([s[i] for s in states], axis=0)

    return (y_prompt, y_sample,
            *[stacked(p_st, i) for i in range(8)],
            *[stacked(s_st, i) for i in range(8)])
```

```python
import functools
import math

import jax
import jax.numpy as jnp
from jax import lax
from jax.experimental import pallas as pl
from jax.experimental.pallas import tpu as pltpu

F32 = jnp.float32
BF16 = jnp.bfloat16

HEAD_DIM = 128
CONV_W = 4
CHUNK = 64
IDX_HEADS = 16
IDX_DIM = 64
TOPK_MAX = 256
Q_BLOCK = 128
NUM_BUCKETS = 32
MAX_DISTANCE = 128
PAGE_SIZE = 128
N_BRANCH = 3
EPS = 1e-6

LANE = 128
VMEM_LIMIT = 56 * 1024 * 1024
DSA_TQ = 256


def _round_up(a, b):
    return (a + b - 1) // b * b


def _rms_rows(x, g):
    return x * lax.rsqrt(jnp.mean(x * x, axis=-1, keepdims=True) + EPS) * g


def _ffn_kernel(x_ref, g_ref, w1_ref, w3_ref, w2_ref, o_ref, n_ref):
    j = pl.program_id(1)

    @pl.when(j == 0)
    def _():
        n_ref[...] = _rms_rows(x_ref[...], g_ref[...]).astype(BF16)
        o_ref[...] = jnp.zeros_like(o_ref)

    n = n_ref[...]
    g = jnp.dot(n, w1_ref[...], preferred_element_type=F32)
    u = jnp.dot(n, w3_ref[...], preferred_element_type=F32)
    a = (g * jax.nn.sigmoid(g) * u).astype(BF16)
    o_ref[...] += jnp.dot(a, w2_ref[...], preferred_element_type=F32)

    @pl.when(j == pl.num_programs(1) - 1)
    def _():
        o_ref[...] = x_ref[...] + 0.5 * o_ref[...]


def _ffn(x, g, w13p, w2p, *, tm=512, tf=512):
    m, d = x.shape
    fp = w2p.shape[0]
    nj = fp // tf
    return pl.pallas_call(
        _ffn_kernel,
        grid=(m // tm, nj),
        in_specs=[
            pl.BlockSpec((tm, d), lambda i, j: (i, 0)),
            pl.BlockSpec((1, d), lambda i, j: (0, 0)),
            pl.BlockSpec((d, tf), lambda i, j: (0, j)),
            pl.BlockSpec((d, tf), lambda i, j: (0, j + nj)),
            pl.BlockSpec((tf, d), lambda i, j: (j, 0)),
        ],
        out_specs=pl.BlockSpec((tm, d), lambda i, j: (i, 0)),
        out_shape=jax.ShapeDtypeStruct((m, d), F32),
        scratch_shapes=[pltpu.VMEM((tm, d), BF16)],
        compiler_params=pltpu.CompilerParams(
            dimension_semantics=("parallel", "arbitrary"), vmem_limit_bytes=VMEM_LIMIT),
        name="ffn",
    )(x, g, w13p, w13p, w2p)


def _inproj_kernel(x_ref, g_ref, w_ref, b_ref, o_ref, n_ref):
    @pl.when(pl.program_id(1) == 0)
    def _():
        n_ref[...] = _rms_rows(x_ref[...], g_ref[...]).astype(BF16)

    o_ref[...] = jnp.dot(n_ref[...], w_ref[...], preferred_element_type=F32) + b_ref[...]


def _inproj(x, g, w, b, *, tm=512, tn=1024):
    m, d = x.shape
    n = w.shape[1]
    return pl.pallas_call(
        _inproj_kernel,
        grid=(m // tm, n // tn),
        in_specs=[
            pl.BlockSpec((tm, d), lambda i, j: (i, 0)),
            pl.BlockSpec((1, d), lambda i, j: (0, 0)),
            pl.BlockSpec((d, tn), lambda i, j: (0, j)),
            pl.BlockSpec((1, tn), lambda i, j: (0, j)),
        ],
        out_specs=pl.BlockSpec((tm, tn), lambda i, j: (i, j)),
        out_shape=jax.ShapeDtypeStruct((m, n), F32),
        scratch_shapes=[pltpu.VMEM((tm, d), BF16)],
        compiler_params=pltpu.CompilerParams(
            dimension_semantics=("parallel", "arbitrary"), vmem_limit_bytes=VMEM_LIMIT),
        name="inproj",
    )(x, g, w, b)


def _inproj_small_kernel(x_ref, g_ref, w_ref, b_ref, lg_ref, lb_ref, o_ref, ki_ref):
    n = _rms_rows(x_ref[...], g_ref[...])
    o = jnp.dot(n, w_ref[...], preferred_element_type=F32, precision=lax.Precision.HIGHEST) + b_ref[...]
    o_ref[...] = o
    ik = o[:, :IDX_DIM]
    mu = jnp.mean(ik, axis=-1, keepdims=True)
    var = jnp.mean(jnp.square(ik - mu), axis=-1, keepdims=True)
    ki_ref[...] = (ik - mu) * lax.rsqrt(var + EPS) * lg_ref[...] + lb_ref[...]


def _inproj_small(x, g, w, b, ln_g, ln_b, *, tm=512):
    m, d = x.shape
    n = w.shape[1]
    return pl.pallas_call(
        _inproj_small_kernel,
        grid=(m // tm,),
        in_specs=[
            pl.BlockSpec((tm, d), lambda i: (i, 0)),
            pl.BlockSpec((1, d), lambda i: (0, 0)),
            pl.BlockSpec((d, n), lambda i: (0, 0)),
            pl.BlockSpec((1, n), lambda i: (0, 0)),
            pl.BlockSpec((1, IDX_DIM), lambda i: (0, 0)),
            pl.BlockSpec((1, IDX_DIM), lambda i: (0, 0)),
        ],
        out_specs=[pl.BlockSpec((tm, n), lambda i: (i, 0)), pl.BlockSpec((tm, IDX_DIM), lambda i: (i, 0))],
        out_shape=[jax.ShapeDtypeStruct((m, n), F32), jax.ShapeDtypeStruct((m, IDX_DIM), F32)],
        compiler_params=pltpu.CompilerParams(
            dimension_semantics=("parallel",), vmem_limit_bytes=VMEM_LIMIT),
        name="inproj_small",
    )(x, g, w, b, ln_g, ln_b)


def _merge_kernel(x_ref, g_ref, oa_ref, ob_ref, oc_ref, wg0_ref, wg1_ref, wg2_ref,
                  bg0_ref, bg1_ref, bg2_ref, wb0_ref, wb1_ref, wb2_ref, wo_ref, o_ref, n_ref):
    j = pl.program_id(1)

    @pl.when(j == 0)
    def _():
        n_ref[...] = _rms_rows(x_ref[...], g_ref[...]).astype(BF16)
        o_ref[...] = jnp.zeros_like(o_ref)

    n = n_ref[...]
    merged = None
    for o_r, wg_r, bg_r, wb_r in ((oa_ref, wg0_ref, bg0_ref, wb0_ref),
                                  (ob_ref, wg1_ref, bg1_ref, wb1_ref),
                                  (oc_ref, wg2_ref, bg2_ref, wb2_ref)):
        gate = jax.nn.sigmoid(jnp.dot(n, wg_r[...], preferred_element_type=F32) + bg_r[...])
        y = jnp.dot(o_r[...], wb_r[0], preferred_element_type=F32)
        merged = gate * y if merged is None else merged + gate * y
    o_ref[...] += jnp.dot(merged.astype(BF16), wo_ref[...], preferred_element_type=F32)

    @pl.when(j == pl.num_programs(1) - 1)
    def _():
        o_ref[...] = x_ref[...] + o_ref[...]


def _merge(x, g, oa, ob, oc, wg, bg, wb, wo, *, tm=512, tn=512):
    m, d = x.shape
    w = oa.shape[1]
    nj = d // tn
    wg_specs = [pl.BlockSpec((d, tn), functools.partial(lambda i, j, n: (0, n * nj + j), n=n))
                for n in range(N_BRANCH)]
    bg_specs = [pl.BlockSpec((1, tn), functools.partial(lambda i, j, n: (0, n * nj + j), n=n))
                for n in range(N_BRANCH)]
    wb_specs = [pl.BlockSpec((1, w, tn), functools.partial(lambda i, j, n: (n, 0, j), n=n))
                for n in range(N_BRANCH)]
    o_spec = pl.BlockSpec((tm, w), lambda i, j: (i, 0))
    return pl.pallas_call(
        _merge_kernel,
        grid=(m // tm, nj),
        in_specs=[pl.BlockSpec((tm, d), lambda i, j: (i, 0)),
                  pl.BlockSpec((1, d), lambda i, j: (0, 0)),
                  o_spec, o_spec, o_spec,
                  *wg_specs, *bg_specs, *wb_specs,
                  pl.BlockSpec((tn, d), lambda i, j: (j, 0))],
        out_specs=pl.BlockSpec((tm, d), lambda i, j: (i, 0)),
        out_shape=jax.ShapeDtypeStruct((m, d), F32),
        scratch_shapes=[pltpu.VMEM((tm, d), BF16)],
        compiler_params=pltpu.CompilerParams(
            dimension_semantics=("parallel", "arbitrary"), vmem_limit_bytes=VMEM_LIMIT),
        name="merge",
    )(x, g, oa, ob, oc, wg, wg, wg, bg, bg, bg, wb, wb, wb, wo)


def _norm_kernel(x_ref, g_ref, o_ref):
    o_ref[...] = _rms_rows(x_ref[...], g_ref[...])


def _final_norm(x, g, *, tm=512):
    m, d = x.shape
    return pl.pallas_call(
        _norm_kernel,
        grid=(m // tm,),
        in_specs=[pl.BlockSpec((tm, d), lambda i: (i, 0)), pl.BlockSpec((1, d), lambda i: (0, 0))],
        out_specs=pl.BlockSpec((tm, d), lambda i: (i, 0)),
        out_shape=jax.ShapeDtypeStruct((m, d), F32),
        compiler_params=pltpu.CompilerParams(dimension_semantics=("parallel",)),
        name="final_norm",
    )(x, g)


NEG = -1e30
LOG2E = math.log2(math.e)
INT_MIN = -2 ** 31


def _order_key(x):
    bits = lax.bitcast_convert_type(x, jnp.int32)
    return bits ^ ((bits >> 31) & jnp.int32(0x7FFFFFFF))


def _kth_largest_key(count_ge, topk, shape):
    t0 = jnp.where(count_ge(jnp.zeros(shape, jnp.int32)) >= topk, jnp.int32(0), jnp.int32(INT_MIN))

    def bit(b, t):
        cand = t + lax.shift_left(jnp.int32(1), jnp.int32(30) - b)
        return jnp.where(count_ge(cand) >= topk, cand, t)

    return lax.fori_loop(0, 31, bit, t0)


def _kth_largest_key_radix(count_ge, topk, shape, bits):
    t = jnp.where(count_ge(jnp.zeros(shape, jnp.int32)) >= topk, jnp.int32(0), jnp.int32(INT_MIN))
    hi = 31
    while hi > 0:
        nb = hi % bits or bits
        sh = hi - nb
        digit = jnp.zeros(shape, jnp.int32)
        for d in range(1, 2 ** nb):
            digit = digit + jnp.where(count_ge(t + jnp.int32(d << sh)) >= topk, 1, 0)
        t = t + lax.shift_left(digit, jnp.int32(sh))
        hi = sh
    return t


def _softmax_update(s, vb, m_prev, l_prev, acc_prev):
    reps = s.shape[1] // LANE
    m_new = jnp.maximum(m_prev, jnp.max(s, axis=1, keepdims=True))
    alpha = jnp.exp2(m_prev - m_new)
    p = jnp.exp2(s - jnp.concatenate([m_new] * reps, axis=1))
    l_new = alpha * l_prev + jnp.sum(p, axis=1, keepdims=True)
    pv = jnp.dot(p.astype(BF16), vb, preferred_element_type=F32)
    dreps = pv.shape[1] // LANE
    acc_new = jnp.concatenate([alpha] * dreps, axis=1) * acc_prev + pv
    return m_new, l_new, acc_new


def _dsa_prompt_kernel(it_ref, jt_ref, qi_ref, w_ref, ki_ref, q_ref, k_ref, v_ref, bd_ref, bs_ref,
                       o_ref, key_ref, thr_ref, wb_ref, qb_ref, m_ref, l_ref, acc_ref, *, topk, tq, nh):
    i = it_ref[pl.program_id(0)]
    j = jt_ref[pl.program_id(0)]
    rg = 128

    @pl.when(j == 0)
    def _select():
        for h in range(IDX_HEADS):
            wb_ref[h] = jnp.broadcast_to(w_ref[:, h:h + 1], (tq, tq))
        qb_ref[...] = (q_ref[...] * (HEAD_DIM ** -0.5 * LOG2E)).astype(BF16)
        m_ref[...] = jnp.full_like(m_ref, NEG)
        l_ref[...] = jnp.zeros_like(l_ref)
        acc_ref[...] = jnp.zeros_like(acc_ref)
        qi = _heads_to_rows(qi_ref)
        row = lax.broadcasted_iota(jnp.int32, (tq, tq), 0)
        col = lax.broadcasted_iota(jnp.int32, (tq, tq), 1)

        def chunk(c, carry):
            kc = ki_ref[pl.ds(pl.multiple_of(c * tq, tq), tq), :]
            s = lax.dot_general(qi, kc, (((1,), (1,)), ((), ())), preferred_element_type=F32)
            sc = jnp.zeros((tq, tq), F32)
            for h in range(IDX_HEADS):
                sc = sc + jnp.maximum(s[h * tq:(h + 1) * tq], 0.0) * wb_ref[h]
            sc = jnp.where(col <= row + jnp.where(c < i, tq, 0), sc, -jnp.inf)
            key_ref[c] = _order_key(sc)
            return carry

        lax.fori_loop(0, i + 1, chunk, 0)

        for r in range(tq // rg):
            rows = pl.ds(r * rg, rg)

            def count_ge(cand):
                def body(c, a):
                    hit = jnp.where(key_ref[c, rows, :] >= cand, 1.0, 0.0)
                    for x in range(tq // LANE):
                        a = a + hit[:, x * LANE:(x + 1) * LANE]
                    return a
                a = lax.fori_loop(0, i + 1, body, jnp.zeros((rg, LANE), F32))
                return jnp.sum(a, axis=1, keepdims=True)

            t = _kth_largest_key(count_ge, float(topk), (rg, 1))
            thr_ref[rows, :] = jnp.broadcast_to(t, (rg, LANE))

    def attend(bias_of_head):
        kk = key_ref[j]
        thr = thr_ref[...]
        selb = jnp.concatenate(
            [jnp.where(kk[:, x * LANE:(x + 1) * LANE] >= thr, 0.0, NEG) for x in range(tq // LANE)], axis=1)
        kb = k_ref[...].astype(BF16)
        vb = v_ref[...].astype(BF16)
        for h in range(nh):
            hs = slice(h * HEAD_DIM, (h + 1) * HEAD_DIM)
            s = lax.dot_general(qb_ref[:, hs], kb[:, hs], (((1,), (1,)), ((), ())),
                                preferred_element_type=F32)
            s = s + selb if bias_of_head is None else s + selb + bias_of_head(h)
            m_new, l_new, acc_new = _softmax_update(s, vb[:, hs], m_ref[h], l_ref[h], acc_ref[:, hs])
            m_ref[h] = m_new
            l_ref[h] = l_new
            acc_ref[:, hs] = acc_new

    @pl.when(j < i - 1)
    def _far():
        attend(None)

    @pl.when(j == i - 1)
    def _sub():
        attend(lambda h: bs_ref[h])

    @pl.when(j == i)
    def _diag():
        attend(lambda h: bd_ref[h])
        for h in range(nh):
            hs = slice(h * HEAD_DIM, (h + 1) * HEAD_DIM)
            o_ref[:, hs] = (acc_ref[:, hs] / l_ref[h]).astype(BF16)


def _t5_bucket(dist):
    n = jnp.maximum(dist, 0)
    max_exact = NUM_BUCKETS // 2
    nf = jnp.maximum(n, 1).astype(F32)
    large = max_exact + (jnp.log(nf / max_exact) / math.log(MAX_DISTANCE / max_exact)
                         * (NUM_BUCKETS - max_exact)).astype(jnp.int32)
    large = jnp.minimum(large, NUM_BUCKETS - 1)
    return jnp.where(n < max_exact, n, large)


def _t5_bias(rel_bias, dist):
    onehot = jax.nn.one_hot(_t5_bucket(dist), NUM_BUCKETS, dtype=F32)
    return jnp.einsum('...b,bh->h...', onehot, rel_bias, precision=lax.Precision.HIGHEST)


def _t5_prompt_tiles(rel_bias, tq):
    assert tq >= MAX_DISTANCE
    r = jnp.arange(tq)[:, None]
    c = jnp.arange(tq)[None, :]
    far = _t5_bias(rel_bias, jnp.int32(tq + 1))[:, None, None]
    bd = jnp.where((r >= c)[None], (_t5_bias(rel_bias, r - c) - far) * LOG2E, NEG)
    bs = (_t5_bias(rel_bias, tq + r - c) - far) * LOG2E
    return bd, bs


def _heads_to_rows(qi_ref):
    return jnp.concatenate([qi_ref[:, h * IDX_DIM:(h + 1) * IDX_DIM] for h in range(IDX_HEADS)],
                           axis=0).astype(BF16)


def _dsa_prompt(big, row0, t, cols, w, ki, rel_bias, tiles=None, *, tq=DSA_TQ):
    bd, bs = _t5_prompt_tiles(rel_bias, tq) if tiles is None else tiles
    nh = rel_bias.shape[1]
    mw = nh * HEAD_DIM
    assert t % tq == 0 and row0 % tq == 0
    nq = t // tq
    rb = row0 // tq
    topk = min(TOPK_MAX, t // 4)
    pairs = [(i, j) for i in range(nq) for j in range(i + 1)]
    it = jnp.asarray([p[0] for p in pairs], jnp.int32)
    jt = jnp.asarray([p[1] for p in pairs], jnp.int32)
    cq, ck, cv, ci = cols
    kern = functools.partial(_dsa_prompt_kernel, topk=topk, tq=tq, nh=nh)
    grid_spec = pltpu.PrefetchScalarGridSpec(
        num_scalar_prefetch=2,
        grid=(len(pairs),),
        in_specs=[
            pl.BlockSpec((tq, IDX_HEADS * IDX_DIM), lambda s, it, jt: (rb + it[s], ci)),
            pl.BlockSpec((tq, IDX_HEADS), lambda s, it, jt: (it[s], 0)),
            pl.BlockSpec((t, IDX_DIM), lambda s, it, jt: (0, 0)),
            pl.BlockSpec((tq, mw), lambda s, it, jt: (rb + it[s], cq)),
            pl.BlockSpec((tq, mw), lambda s, it, jt: (rb + jt[s], ck)),
            pl.BlockSpec((tq, mw), lambda s, it, jt: (rb + jt[s], cv)),
            pl.BlockSpec((nh, tq, tq), lambda s, it, jt: (0, 0, 0)),
            pl.BlockSpec((nh, tq, tq), lambda s, it, jt: (0, 0, 0)),
        ],
        out_specs=pl.BlockSpec((tq, mw), lambda s, it, jt: (it[s], 0)),
        scratch_shapes=[
            pltpu.VMEM((nq, tq, tq), jnp.int32),
            pltpu.VMEM((tq, LANE), jnp.int32),
            pltpu.VMEM((IDX_HEADS, tq, tq), F32),
            pltpu.VMEM((tq, mw), BF16),
            pltpu.VMEM((nh, tq, LANE), F32),
            pltpu.VMEM((nh, tq, LANE), F32),
            pltpu.VMEM((tq, mw), F32),
        ],
    )
    return pl.pallas_call(
        kern,
        grid_spec=grid_spec,
        out_shape=jax.ShapeDtypeStruct((t, mw), BF16),
        compiler_params=pltpu.CompilerParams(
            dimension_semantics=("arbitrary",), vmem_limit_bytes=VMEM_LIMIT),
        name="dsa_prompt",
    )(it, jt, big, w, ki, big, big, big, bd, bs)


def _dsa_sample_kernel(pt_ref, qi_ref, w_ref, kin_ref, q_ref, kn_ref, vn_ref, bias_ref, biasn_ref, *rest,
                       topk, t, n_pages, g_pages, nh, past):
    idx_refs = rest[:n_pages]
    kp_refs = rest[n_pages:n_pages + g_pages]
    vp_refs = rest[n_pages + g_pages:n_pages + 2 * g_pages]
    o_ref, selb_ref, selbn_ref, qbd_ref, s_ref, vb_ref = rest[n_pages + 2 * g_pages:]
    g = pl.program_id(1)
    ng = n_pages // g_pages
    gw = g_pages * PAGE_SIZE
    rows = nh * t
    mw = nh * HEAD_DIM

    @pl.when(g == 0)
    def _select():
        qi = _heads_to_rows(qi_ref)
        wcol = w_ref[...]
        pad = jnp.zeros((PAGE_SIZE - t, IDX_DIM), F32)
        keys = jnp.concatenate([r[...] for r in idx_refs] + [kin_ref[...], pad], axis=0).astype(BF16)
        s = lax.dot_general(qi, keys, (((1,), (1,)), ((), ())), preferred_element_type=F32)
        s = jnp.maximum(s, 0.0) * wcol
        sc = s[0:t]
        for h in range(1, IDX_HEADS):
            sc = sc + s[h * t:(h + 1) * t]
        kpos = lax.broadcasted_iota(jnp.int32, sc.shape, 1)
        qrow = lax.broadcasted_iota(jnp.int32, sc.shape, 0)
        key = _order_key(jnp.where(kpos <= past + qrow, sc, -jnp.inf))

        def count_ge(cand):
            return jnp.sum(jnp.where(key >= cand, 1.0, 0.0), axis=1, keepdims=True)

        thr = _kth_largest_key_radix(count_ge, float(topk), (t, 1), 4)
        selb = jnp.where(key >= thr, 0.0, NEG)
        for x in range(ng):
            selb_ref[x] = selb[:, x * gw:(x + 1) * gw]
        selbn_ref[...] = selb[:, past:]

        qs = jnp.concatenate([q_ref[...] * (HEAD_DIM ** -0.5 * LOG2E)] * nh, axis=0)
        rr = lax.broadcasted_iota(jnp.int32, (rows, mw), 0) >> (t.bit_length() - 1)
        cc = lax.broadcasted_iota(jnp.int32, (rows, mw), 1) >> (HEAD_DIM.bit_length() - 1)
        qbd_ref[...] = jnp.where(rr == cc, qs, 0.0).astype(BF16)

    def logits(kb, bias):
        return lax.dot_general(qbd_ref[...], kb, (((1,), (1,)), ((), ())), preferred_element_type=F32) + bias

    def page(r):
        return jnp.concatenate([r[pl.ds(h, PAGE_SIZE, stride=nh), :] for h in range(nh)], axis=1)

    for x in range(g_pages):
        ps = slice(x * PAGE_SIZE, (x + 1) * PAGE_SIZE)
        bias = bias_ref[g, :, ps] + jnp.concatenate([selb_ref[g, :, ps]] * nh, axis=0)
        s_ref[g, :, ps] = logits(page(kp_refs[x]).astype(BF16), bias)
        vb_ref[g, ps, :] = page(vp_refs[x]).astype(BF16)

    @pl.when(g == ng - 1)
    def _finish():
        padn = jnp.zeros((PAGE_SIZE - t, mw), F32)
        kn = jnp.concatenate([kn_ref[...], padn], axis=0).astype(BF16)
        vn = jnp.concatenate([vn_ref[...], padn], axis=0).astype(BF16)
        s_new = logits(kn, biasn_ref[...] + jnp.concatenate([selbn_ref[...]] * nh, axis=0))
        s_past = [s_ref[x] for x in range(ng)]
        m = jnp.max(s_new, axis=1, keepdims=True)
        for x in range(ng):
            m = jnp.maximum(m, jnp.max(s_past[x], axis=1, keepdims=True))
        p_new = jnp.exp2(s_new - m)
        den = jnp.sum(p_new, axis=1, keepdims=True)
        acc = jnp.dot(p_new.astype(BF16), vn, preferred_element_type=F32)
        for x in range(ng):
            p = jnp.exp2(s_past[x] - m)
            den = den + jnp.sum(p, axis=1, keepdims=True)
            acc = acc + jnp.dot(p.astype(BF16), vb_ref[x], preferred_element_type=F32)
        for h in range(nh):
            hs = slice(h * HEAD_DIM, (h + 1) * HEAD_DIM)
            rs = slice(h * t, (h + 1) * t)
            o_ref[:, hs] = (acc[rs, hs] / den[rs, :]).astype(BF16)


def _dsa_sample(big, row0, b, t, cols, w, kin, cache_k, cache_v, cache_idx_k, l, page_table, rel_bias):
    nh = rel_bias.shape[1]
    mw = nh * HEAD_DIM
    n_pages = page_table.shape[1]
    past = n_pages * PAGE_SIZE
    g_pages = math.gcd(n_pages, 4)
    ng = n_pages // g_pages
    gw = g_pages * PAGE_SIZE
    topk = min(TOPK_MAX, (past + t) // 4)
    assert row0 % t == 0 and t % 8 == 0 and t <= PAGE_SIZE and t & (t - 1) == 0
    rb = row0 // t
    rows = nh * t
    qpos = past + jnp.arange(t)
    kpos = jnp.arange(past + PAGE_SIZE)
    dist = qpos[:, None] - kpos[None, :]
    bias = jnp.where((dist >= 0)[None], _t5_bias(rel_bias, dist) * LOG2E, NEG)
    bias = bias.reshape(rows, past + PAGE_SIZE)
    bias_past = bias[:, :past].reshape(rows, ng, gw).transpose(1, 0, 2)
    bias_new = bias[:, past:]
    cq, ck, cv, ci = cols

    def page_map(x, bi, gi, pt):
        return (l, pt[bi, gi * g_pages + x], 0, 0)

    def idx_map(x, bi, gi, pt):
        return (l, pt[bi, x], 0, 0)

    in_specs = [
        pl.BlockSpec((t, IDX_HEADS * IDX_DIM), lambda bi, gi, pt: (rb + bi, ci)),
        pl.BlockSpec((None, IDX_HEADS * t, 1), lambda bi, gi, pt: (bi, 0, 0)),
        pl.BlockSpec((None, t, IDX_DIM), lambda bi, gi, pt: (bi, 0, 0)),
        pl.BlockSpec((t, mw), lambda bi, gi, pt: (rb + bi, cq)),
        pl.BlockSpec((t, mw), lambda bi, gi, pt: (rb + bi, ck)),
        pl.BlockSpec((t, mw), lambda bi, gi, pt: (rb + bi, cv)),
        pl.BlockSpec((ng, rows, gw), lambda bi, gi, pt: (0, 0, 0)),
        pl.BlockSpec((rows, PAGE_SIZE), lambda bi, gi, pt: (0, 0)),
    ]
    in_specs += [pl.BlockSpec((None, None, PAGE_SIZE, IDX_DIM), functools.partial(idx_map, x))
                 for x in range(n_pages)]
    in_specs += [pl.BlockSpec((None, None, PAGE_SIZE * nh, HEAD_DIM), functools.partial(page_map, x % g_pages))
                 for x in range(2 * g_pages)]
    kern = functools.partial(_dsa_sample_kernel, topk=topk, t=t, n_pages=n_pages, g_pages=g_pages,
                             nh=nh, past=past)
    grid_spec = pltpu.PrefetchScalarGridSpec(
        num_scalar_prefetch=1,
        grid=(b, ng),
        in_specs=in_specs,
        out_specs=pl.BlockSpec((t, mw), lambda bi, gi, pt: (bi, 0)),
        scratch_shapes=[
            pltpu.VMEM((ng, t, gw), F32),
            pltpu.VMEM((t, PAGE_SIZE), F32),
            pltpu.VMEM((rows, mw), BF16),
            pltpu.VMEM((ng, rows, gw), F32),
            pltpu.VMEM((ng, gw, mw), BF16),
        ],
    )
    return pl.pallas_call(
        kern,
        grid_spec=grid_spec,
        out_shape=jax.ShapeDtypeStruct((b * t, mw), BF16),
        compiler_params=pltpu.CompilerParams(
            dimension_semantics=("parallel", "arbitrary"), vmem_limit_bytes=VMEM_LIMIT),
        name="dsa_sample",
    )(page_table, big, w, kin, big, big, big, bias_past, bias_new,
      *([cache_idx_k] * n_pages), *([cache_k] * g_pages), *([cache_v] * g_pages))


def _lane_expand(x, nh):
    c = x.shape[0]
    return jnp.concatenate([jnp.broadcast_to(x[:, h:h + 1], (c, HEAD_DIM)) for h in range(nh)], axis=1)


def _bdot(a, b):
    return jnp.dot(a.astype(BF16), b.astype(BF16), preferred_element_type=F32)


def _dot01(a01, b):
    a = a01.astype(BF16)
    out = None
    rest = b
    for _ in range(3):
        part = rest.astype(BF16)
        rest = rest - part.astype(F32)
        term = jnp.dot(a, part, preferred_element_type=F32)
        out = term if out is None else out + term
    return out


def _bdot_nt(a, b):
    return lax.dot_general(a.astype(BF16), b.astype(BF16), (((1,), (1,)), ((), ())), preferred_element_type=F32)


def _bdot_tn(a, b):
    return lax.dot_general(a.astype(BF16), b.astype(BF16), (((0,), (0,)), ((), ())), preferred_element_type=F32)


def _softplus(x):
    return jnp.maximum(x, 0.0) + jnp.log(1.0 + jnp.exp(-jnp.abs(x)))


def _gdn_kernel(qkv_ref, z_ref, sm_ref, cw_ref, alog_ref, dtb_ref, ng_ref, s0_ref, conv0_ref,
                o_ref, s_out_ref, conv_out_ref, s_ref, xp_ref, *, c, nh, ob, oa):
    n = pl.program_id(1)
    mw = nh * HEAD_DIM
    halo = 8

    @pl.when(n == 0)
    def _():
        s_ref[...] = s0_ref[...]
        xp_ref[0:halo, :] = conv0_ref[...]

    xp_ref[halo:halo + c, :] = qkv_ref[...]
    y = cw_ref[0:1, :] * xp_ref[halo - CONV_W + 1:halo - CONV_W + 1 + c, :]
    for j in range(1, CONV_W):
        y = y + cw_ref[j:j + 1, :] * xp_ref[halo - CONV_W + 1 + j:halo - CONV_W + 1 + j + c, :]
    act = y * jax.nn.sigmoid(y)

    sm = sm_ref[...]
    beta = jax.nn.sigmoid(sm[:, ob:ob + nh])
    g = -jnp.exp(alog_ref[...]) * _softplus(sm[:, oa:oa + nh] + dtb_ref[...])
    beta_e = _lane_expand(beta, nh)
    g_e = _lane_expand(g, nh)
    t_i = lax.broadcasted_iota(jnp.int32, (c, mw), 0)
    j_i = lax.broadcasted_iota(jnp.int32, (c, mw), 1) & (HEAD_DIM - 1)
    ii = lax.broadcasted_iota(jnp.int32, (c, c), 0)
    jj = lax.broadcasted_iota(jnp.int32, (c, c), 1)
    tri = jnp.where(jj <= ii, 1.0, 0.0)
    cum = _dot01(tri, jnp.concatenate([jnp.where(t_i > j_i, g_e, 0.0), g, jnp.zeros((c, LANE - nh), F32)], axis=1))
    dmat = cum[:, :mw]
    gc = _lane_expand(cum[:, mw:mw + nh], nh)
    g_end = gc[c - 1:c, :]
    e_gc = jnp.exp(gc)
    e_rest = jnp.exp(g_end - gc)
    e_end = jnp.exp(g_end)
    eye = jnp.where(ii == jj, 1.0, 0.0)

    heads = range(nh)
    hsl = [slice(h * HEAD_DIM, (h + 1) * HEAD_DIM) for h in heads]
    qn, kn, kb, dec = [], [], [], []
    for h in heads:
        qh = act[:, h * HEAD_DIM:(h + 1) * HEAD_DIM]
        kh = act[:, mw + h * HEAD_DIM:mw + (h + 1) * HEAD_DIM]
        qn.append(qh * lax.rsqrt(jnp.sum(qh * qh, axis=1, keepdims=True) + EPS) * HEAD_DIM ** -0.5)
        kn.append(kh * lax.rsqrt(jnp.sum(kh * kh, axis=1, keepdims=True) + EPS))
        kb.append(kn[h] * beta_e[:, hsl[h]])
        dec.append(jnp.where(ii >= jj, jnp.exp(dmat[:, h * HEAD_DIM:h * HEAD_DIM + c]), 0.0))
    knb = [kn[h].astype(BF16) for h in heads]
    kk = [_bdot_nt(kb[h], knb[h]) for h in heads]
    qk = [_bdot_nt(qn[h], knb[h]) for h in heads]
    x = [jnp.where(ii > jj, -kk[h] * dec[h], 0.0) for h in heads]
    p = [eye + x[h] for h in heads]
    for _ in range(c.bit_length() - 2):
        x = [_bdot(x[h], x[h]) for h in heads]
        p = [p[h] + _bdot(p[h], x[h]) for h in heads]
    uw = [_bdot(p[h], jnp.concatenate(
        [act[:, 2 * mw + h * HEAD_DIM:2 * mw + (h + 1) * HEAD_DIM] * beta_e[:, hsl[h]],
         kb[h] * e_gc[:, hsl[h]]], axis=1)) for h in heads]
    s_old = [s_ref[h] for h in heads]
    sb = [s_old[h].astype(BF16) for h in heads]
    o_inter = [_bdot(qn[h] * e_gc[:, hsl[h]], sb[h]) for h in heads]
    v_new = [uw[h][:, :HEAD_DIM] - _bdot(uw[h][:, HEAD_DIM:], sb[h]) for h in heads]
    o_intra = [_bdot(qk[h] * dec[h], v_new[h]) for h in heads]
    kv = [_bdot_tn(kn[h] * e_rest[:, hsl[h]], v_new[h]) for h in heads]
    for h in heads:
        s_ref[h] = s_old[h] * e_end[:, hsl[h]] + kv[h]
        o = o_inter[h] + o_intra[h]
        on = o * lax.rsqrt(jnp.mean(o * o, axis=1, keepdims=True) + EPS) * ng_ref[...]
        zh = z_ref[:, hsl[h]]
        o_ref[:, hsl[h]] = (on * zh * jax.nn.sigmoid(zh)).astype(o_ref.dtype)

    tail = xp_ref[c:c + halo, :]
    xp_ref[0:halo, :] = tail

    @pl.when(n == pl.num_programs(1) - 1)
    def _():
        s_out_ref[...] = s_ref[...]
        conv_out_ref[...] = tail


def _gdn(big, small, row0, b, t, cq, cz, ob, oa, conv_w, a_log, dt_bias, norm_g, s0, conv0):
    nh = s0.shape[1]
    mw = nh * HEAD_DIM
    c = math.gcd(t, CHUNK)
    assert c % 8 == 0 and c & (c - 1) == 0 and row0 % c == 0
    nc = t // c
    rb = row0 // c
    halo = 8
    conv0p = jnp.pad(conv0, ((0, 0), (halo - (CONV_W - 1), 0), (0, 0)))
    odt = BF16 if c % 16 == 0 else F32
    kern = functools.partial(_gdn_kernel, c=c, nh=nh, ob=ob, oa=oa)
    o, s_new, conv_new = pl.pallas_call(
        kern,
        grid=(b, nc),
        in_specs=[
            pl.BlockSpec((c, 3 * mw), lambda bi, n: (rb + bi * nc + n, cq)),
            pl.BlockSpec((c, mw), lambda bi, n: (rb + bi * nc + n, cz)),
            pl.BlockSpec((c, LANE), lambda bi, n: (rb + bi * nc + n, 0)),
            pl.BlockSpec((CONV_W, 3 * mw), lambda bi, n: (0, 0)),
            pl.BlockSpec((1, nh), lambda bi, n: (0, 0)),
            pl.BlockSpec((1, nh), lambda bi, n: (0, 0)),
            pl.BlockSpec((1, HEAD_DIM), lambda bi, n: (0, 0)),
            pl.BlockSpec((None, nh, HEAD_DIM, HEAD_DIM), lambda bi, n: (bi, 0, 0, 0)),
            pl.BlockSpec((None, halo, 3 * mw), lambda bi, n: (bi, 0, 0)),
        ],
        out_specs=[
            pl.BlockSpec((c, mw), lambda bi, n: (bi * nc + n, 0)),
            pl.BlockSpec((None, nh, HEAD_DIM, HEAD_DIM), lambda bi, n: (bi, 0, 0, 0)),
            pl.BlockSpec((None, halo, 3 * mw), lambda bi, n: (bi, 0, 0)),
        ],
        out_shape=[
            jax.ShapeDtypeStruct((b * t, mw), odt),
            jax.ShapeDtypeStruct((b, nh, HEAD_DIM, HEAD_DIM), F32),
            jax.ShapeDtypeStruct((b, halo, 3 * mw), F32),
        ],
        scratch_shapes=[pltpu.VMEM((nh, HEAD_DIM, HEAD_DIM), F32), pltpu.VMEM((halo + c, 3 * mw), F32)],
        compiler_params=pltpu.CompilerParams(
            dimension_semantics=("parallel", "arbitrary"), vmem_limit_bytes=VMEM_LIMIT),
        name="gdn",
    )(big, big, small, conv_w, a_log[None], dt_bias[None], norm_g[None], s0, conv0p)
    return o, s_new, conv_new[:, halo - (CONV_W - 1):]


def _mlstm_kernel(q_ref, k_ref, v_ref, og_ref, sm_ref, ng_ref, c0_ref, n0_ref, m0_ref,
                  o_ref, c_out_ref, n_out_ref, m_out_ref, c_ref, n_ref, m_ref, *, c, nh, oi, of):
    n = pl.program_id(1)
    mw = nh * HEAD_DIM

    @pl.when(n == 0)
    def _():
        c_ref[...] = c0_ref[...]
        n_ref[...] = n0_ref[...]
        m_ref[...] = m0_ref[...]

    sm = sm_ref[...]
    ig_e = _lane_expand(sm[:, oi:oi + nh], nh)
    lf_e = _lane_expand(-_softplus(-sm[:, of:of + nh]), nh)
    t_i = lax.broadcasted_iota(jnp.int32, (c, mw), 0)
    j_i = lax.broadcasted_iota(jnp.int32, (c, mw), 1) & (HEAD_DIM - 1)
    ii = lax.broadcasted_iota(jnp.int32, (c, c), 0)
    jj = lax.broadcasted_iota(jnp.int32, (c, c), 1)
    lhs = jnp.concatenate([jnp.where(jj <= ii, 1.0, 0.0), jnp.ones((c, c), F32)], axis=1)
    rhs = jnp.concatenate([
        jnp.concatenate([lf_e, jnp.where(t_i > j_i, lf_e, 0.0)], axis=1),
        jnp.concatenate([jnp.zeros((c, mw), F32), jnp.where(t_i == j_i, ig_e, 0.0)], axis=1)], axis=0)
    cum = jnp.dot(lhs, rhs, preferred_element_type=F32, precision=lax.Precision.HIGHEST)
    bc = cum[:, :mw]
    dlog = cum[:, mw:]
    b_end = bc[c - 1:c, :]
    a_end = b_end - bc + ig_e

    heads = range(nh)
    hsl = [slice(h * HEAD_DIM, (h + 1) * HEAD_DIM) for h in heads]
    qb = [(q_ref[:, hsl[h]] * HEAD_DIM ** -0.5).astype(BF16) for h in heads]
    kh = [k_ref[:, hsl[h]] for h in heads]
    vb = [v_ref[:, hsl[h]].astype(BF16) for h in heads]
    cs = [c_ref[h] for h in heads]
    qk = [_bdot_nt(qb[h], kh[h]) for h in heads]
    qc = [_bdot(qb[h], cs[h]) for h in heads]
    m_prev = [m_ref[h:h + 1, :] for h in heads]
    m_new = [jnp.maximum(b_end[:, hsl[h]] + m_prev[h], jnp.max(a_end[:, hsl[h]], axis=0, keepdims=True))
             for h in heads]
    kw = [kh[h] * jnp.exp(a_end[:, hsl[h]] - m_new[h]) for h in heads]
    kv = [_bdot_tn(kw[h], vb[h]) for h in heads]
    mt, sc = [], []
    for h in heads:
        dl = jnp.where(ii >= jj, dlog[:, h * HEAD_DIM:h * HEAD_DIM + c], -jnp.inf)
        inter = bc[:, hsl[h]] + m_prev[h]
        mt.append(jnp.maximum(inter, jnp.max(dl, axis=1, keepdims=True)))
        sc.append(qk[h] * jnp.exp(dl - mt[h][:, :c]))
    sv = [_bdot(sc[h], vb[h]) for h in heads]
    for h in heads:
        w_inter = jnp.exp(bc[:, hsl[h]] + m_prev[h] - mt[h])
        ns = n_ref[h:h + 1, :]
        qf = q_ref[:, hsl[h]] * HEAD_DIM ** -0.5
        num = w_inter * qc[h] + sv[h]
        den = w_inter * jnp.sum(qf * ns, axis=1, keepdims=True) + jnp.sum(sc[h], axis=1, keepdims=True)
        hh = num / jnp.maximum(jnp.abs(den), jnp.exp(-mt[h]))
        keep = jnp.exp(b_end[:, hsl[h]] + m_prev[h] - m_new[h])
        c_ref[h] = keep * cs[h] + kv[h]
        n_ref[h:h + 1, :] = keep * ns + jnp.sum(kw[h], axis=0, keepdims=True)
        m_ref[h:h + 1, :] = m_new[h]
        hn = hh * lax.rsqrt(jnp.mean(hh * hh, axis=1, keepdims=True) + EPS) * ng_ref[...]
        o_ref[:, hsl[h]] = (hn * jax.nn.sigmoid(og_ref[:, hsl[h]])).astype(o_ref.dtype)

    @pl.when(n == pl.num_programs(1) - 1)
    def _():
        c_out_ref[...] = c_ref[...]
        n_out_ref[...] = n_ref[...]
        m_out_ref[...] = m_ref[...]


def _mlstm(big, small, row0, b, t, cols, oi, of, norm_g, c0, n0, m0):
    nh = c0.shape[1]
    mw = nh * HEAD_DIM
    c = math.gcd(t, CHUNK)
    assert c % 8 == 0 and c <= HEAD_DIM and row0 % c == 0
    nc = t // c
    rb = row0 // c
    odt = BF16 if c % 16 == 0 else F32
    m0e = jnp.broadcast_to(m0[:, :, None], (b, nh, LANE))

    def col_spec(cc):
        return pl.BlockSpec((c, mw), lambda bi, n: (rb + bi * nc + n, cc))

    kern = functools.partial(_mlstm_kernel, c=c, nh=nh, oi=oi, of=of)
    o, c_new, n_new, m_new = pl.pallas_call(
        kern,
        grid=(b, nc),
        in_specs=[
            *[col_spec(cc) for cc in cols],
            pl.BlockSpec((c, LANE), lambda bi, n: (rb + bi * nc + n, 0)),
            pl.BlockSpec((1, HEAD_DIM), lambda bi, n: (0, 0)),
            pl.BlockSpec((None, nh, HEAD_DIM, HEAD_DIM), lambda bi, n: (bi, 0, 0, 0)),
            pl.BlockSpec((None, nh, HEAD_DIM), lambda bi, n: (bi, 0, 0)),
            pl.BlockSpec((None, nh, LANE), lambda bi, n: (bi, 0, 0)),
        ],
        out_specs=[
            pl.BlockSpec((c, mw), lambda bi, n: (bi * nc + n, 0)),
            pl.BlockSpec((None, nh, HEAD_DIM, HEAD_DIM), lambda bi, n: (bi, 0, 0, 0)),
            pl.BlockSpec((None, nh, HEAD_DIM), lambda bi, n: (bi, 0, 0)),
            pl.BlockSpec((None, nh, LANE), lambda bi, n: (bi, 0, 0)),
        ],
        out_shape=[
            jax.ShapeDtypeStruct((b * t, mw), odt),
            jax.ShapeDtypeStruct((b, nh, HEAD_DIM, HEAD_DIM), F32),
            jax.ShapeDtypeStruct((b, nh, HEAD_DIM), F32),
            jax.ShapeDtypeStruct((b, nh, LANE), F32),
        ],
        scratch_shapes=[pltpu.VMEM((nh, HEAD_DIM, HEAD_DIM), F32), pltpu.VMEM((nh, HEAD_DIM), F32),
                        pltpu.VMEM((nh, LANE), F32)],
        compiler_params=pltpu.CompilerParams(
            dimension_semantics=("parallel", "arbitrary"), vmem_limit_bytes=VMEM_LIMIT),
        name="mlstm",
    )(big, big, big, big, small, norm_g[None], c0, n0, m0e)
    return o, c_new, n_new, m_new[:, :, 0]


def _segments(d_model):
    mw = d_model // 2
    nh = mw // HEAD_DIM
    widths = (3 * mw, mw, nh, nh, mw, mw, mw, mw, nh, nh, mw, mw, mw,
              IDX_HEADS * IDX_DIM, IDX_HEADS, IDX_DIM, N_BRANCH * d_model)
    offs = [0]
    for w in widths:
        offs.append(offs[-1] + w)
    names = ('g_qkv', 'g_z', 'g_b', 'g_a', 'm_q', 'm_k', 'm_v', 'm_o', 'm_i', 'm_f',
             'a_q', 'a_k', 'a_v', 'i_q', 'i_w', 'i_k', 'gates')
    return {nm: (offs[i], widths[i]) for i, nm in enumerate(names)}


_BIG = ('g_qkv', 'g_z', 'm_q', 'm_k', 'm_v', 'm_o', 'a_q', 'a_k', 'a_v', 'i_q')
_SMALL = ('i_k', 'g_b', 'g_a', 'm_i', 'm_f', 'i_w')


def _take_cols(a, seg, names):
    return jnp.concatenate([a[..., seg[n][0]:seg[n][0] + seg[n][1]] for n in names], axis=-1)


def _layout(seg, names):
    out, off = {}, 0
    for n in names:
        out[n] = (off, seg[n][1])
        off += seg[n][1]
    return out, off


def _trunk_layer(xs, l, P, rel_bias, sample_ctx, n_prompt, prompt_shape, sample_shape, init_sample):
    d = xs.shape[1]
    mw = d // 2
    nh = mw // HEAD_DIM
    bp, tp = prompt_shape
    bs, ts = sample_shape

    x1 = _ffn(xs, P['norm_g'][l, 0][None], P['ffa_w13'][l], P['ffa_w2'][l])
    g1 = P['norm_g'][l, 1][None]
    big = _inproj(x1, g1, P['w_big'][l], P['b_big'][l][None])
    small, kidx = _inproj_small(x1, g1, P['w_small'][l], P['b_small'][l][None],
                                P['idx_ln_g'][l][None], P['idx_ln_b'][l][None])
    lb, ls = P['lay_big'], P['lay_small']
    att_cols = tuple(lb[n][0] // mw for n in ('a_q', 'a_k', 'a_v', 'i_q'))
    idx_scale = IDX_DIM ** -0.5 * IDX_HEADS ** -0.5

    outs = []
    states = []
    for grp in range(2):
        if grp == 0:
            b, t = bp, tp
            init = (jnp.zeros((bp, nh, HEAD_DIM, HEAD_DIM), F32), jnp.zeros((bp, CONV_W - 1, 3 * mw), F32),
                    jnp.zeros((bp, nh, HEAD_DIM, HEAD_DIM), F32), jnp.zeros((bp, nh, HEAD_DIM), F32),
                    jnp.zeros((bp, nh), F32))
        else:
            b, t = bs, ts
            init = tuple(a[l] for a in init_sample)
        s0, conv0, c0, n0, m0 = init
        row0 = 0 if grp == 0 else n_prompt

        def seg_b(name):
            o, w = lb[name]
            return lax.slice(big, (row0, o), (row0 + b * t, o + w))

        def seg_s(name):
            o, w = ls[name]
            return lax.slice(small, (row0, o), (row0 + b * t, o + w))

        o_a, s_new, conv_new = _gdn(
            big, small, row0, b, t, lb['g_qkv'][0] // (3 * mw), lb['g_z'][0] // mw, ls['g_b'][0], ls['g_a'][0],
            P['gdn_conv_w'][l], P['gdn_a_log'][l], P['gdn_dt_bias'][l], P['gdn_norm_g'][l], s0, conv0)
        o_b, c_new, n_new, m_new = _mlstm(
            big, small, row0, b, t, tuple(lb[nm][0] // mw for nm in ('m_q', 'm_k', 'm_v', 'm_o')),
            ls['m_i'][0], ls['m_f'][0], P['mlstm_norm_g'][l], c0, n0, m0)
        k_att = seg_b('a_k').reshape(b, t, nh, HEAD_DIM)
        v_att = seg_b('a_v').reshape(b, t, nh, HEAD_DIM)
        k_idx = kidx[row0:row0 + b * t].reshape(b, t, IDX_DIM)
        w_idx = seg_s('i_w').reshape(b, t, IDX_HEADS) * idx_scale
        if grp == 0:
            o_c = jnp.concatenate([
                _dsa_prompt(big, row0 + bi * t, t, att_cols, w_idx[bi], k_idx[bi].astype(BF16),
                            rel_bias, P['t5_tiles']) for bi in range(b)], axis=0)
        else:
            cache_k, cache_v, cache_idx_k, page_table = sample_ctx
            o_c = _dsa_sample(big, row0, b, t, att_cols,
                              w_idx.transpose(0, 2, 1).reshape(b, IDX_HEADS * t, 1), k_idx,
                              cache_k.reshape(*cache_k.shape[:2], PAGE_SIZE * nh, HEAD_DIM),
                              cache_v.reshape(*cache_v.shape[:2], PAGE_SIZE * nh, HEAD_DIM),
                              cache_idx_k, l, page_table, rel_bias)
        outs.append((o_a.reshape(b * t, mw).astype(BF16), o_b.reshape(b * t, mw).astype(BF16), o_c))
        states.append((k_att, v_att, k_idx, s_new, conv_new, c_new, n_new, m_new))

    oa = jnp.concatenate([outs[0][0], outs[1][0]], axis=0)
    ob = jnp.concatenate([outs[0][1], outs[1][1]], axis=0)
    oc = jnp.concatenate([outs[0][2], outs[1][2]], axis=0)
    x2 = _merge(x1, g1, oa, ob, oc, P['w_gates'][l], P['b_gates'][l][None], P['w_branch'][l], P['w_out'][l])
    x3 = _ffn(x2, P['norm_g'][l, 2][None], P['ffb_w13'][l], P['ffb_w2'][l])
    return x3, states


def _repack_kernel(x_ref, o_ref, *, nb_valid, nb_padded):
    jb = pl.program_id(1) % nb_padded
    o_ref[...] = jnp.where(jb < nb_valid, x_ref[...], 0.0).astype(o_ref.dtype)


def _repack_bf16(w, axis, f, fp):
    nl, a, b = w.shape
    groups = w.shape[axis] // f
    assert f % LANE == 0 and fp % LANE == 0 and groups * f == w.shape[axis]
    nbv, nbp = f // LANE, fp // LANE

    def src_block(j):
        return (j // nbp) * nbv + jnp.minimum(j % nbp, nbv - 1)

    if axis == 2:
        in_spec = pl.BlockSpec((None, a, LANE), lambda l, j: (l, 0, src_block(j)))
        out_spec = pl.BlockSpec((None, a, LANE), lambda l, j: (l, 0, j))
        out_shape = (nl, a, groups * fp)
    else:
        in_spec = pl.BlockSpec((None, LANE, b), lambda l, j: (l, src_block(j), 0))
        out_spec = pl.BlockSpec((None, LANE, b), lambda l, j: (l, j, 0))
        out_shape = (nl, groups * fp, b)
    return pl.pallas_call(
        functools.partial(_repack_kernel, nb_valid=nbv, nb_padded=nbp),
        grid=(nl, groups * nbp),
        in_specs=[in_spec],
        out_specs=out_spec,
        out_shape=jax.ShapeDtypeStruct(out_shape, BF16),
        compiler_params=pltpu.CompilerParams(dimension_semantics=("parallel", "arbitrary")),
        name="repack",
    )(w)


def _prep_ffn(w13, w2, tf=512):
    f = w2.shape[1]
    fp = _round_up(f, tf)
    return _repack_bf16(w13, 2, f, fp), _repack_bf16(w2, 1, f, fp)


def kernel(x_prompt, x_sample, cache_k, cache_v, cache_idx_k, state_gdn, state_gdn_conv, state_mlstm_c,
           state_mlstm_n, state_mlstm_m, page_table, norm_g, final_g, ffa_w13, ffa_w2, ffb_w13, ffb_w2,
           w_in, b_in, gdn_conv_w, gdn_a_log, gdn_dt_bias, gdn_norm_g, mlstm_norm_g, idx_ln_g, idx_ln_b,
           rel_bias, w_branch, w_out):
    bp, tp, d = x_prompt.shape
    bs, ts, _ = x_sample.shape
    depth = w_in.shape[0]
    seg = _segments(d)
    lay_big, _ = _layout(seg, _BIG)
    lay_small, n_small = _layout(seg, _SMALL)
    pad_small = _round_up(n_small, LANE) - n_small

    ffa_w13p, ffa_w2p = _prep_ffn(ffa_w13, ffa_w2)
    ffb_w13p, ffb_w2p = _prep_ffn(ffb_w13, ffb_w2)
    go, gw = seg['gates']
    w_in_b = lax.optimization_barrier(w_in.astype(BF16))
    P = dict(
        norm_g=norm_g,
        ffa_w13=ffa_w13p, ffa_w2=ffa_w2p,
        ffb_w13=ffb_w13p, ffb_w2=ffb_w2p,
        w_big=_take_cols(w_in_b, seg, _BIG), b_big=_take_cols(b_in, seg, _BIG),
        w_small=jnp.pad(_take_cols(w_in, seg, _SMALL), ((0, 0), (0, 0), (0, pad_small))),
        b_small=jnp.pad(_take_cols(b_in, seg, _SMALL), ((0, 0), (0, pad_small))),
        w_gates=w_in_b[:, :, go:go + gw], b_gates=b_in[:, go:go + gw],
        w_branch=w_branch.astype(BF16), w_out=w_out.astype(BF16),
        gdn_conv_w=gdn_conv_w, gdn_a_log=gdn_a_log, gdn_dt_bias=gdn_dt_bias, gdn_norm_g=gdn_norm_g,
        mlstm_norm_g=mlstm_norm_g, idx_ln_g=idx_ln_g, idx_ln_b=idx_ln_b,
        lay_big=lay_big, lay_small=lay_small,
        t5_tiles=_t5_prompt_tiles(rel_bias, DSA_TQ),
    )

    n_prompt = bp * tp
    xs = jnp.concatenate([x_prompt.reshape(n_prompt, d), x_sample.reshape(bs * ts, d)], axis=0)
    init_sample = (state_gdn, state_gdn_conv, state_mlstm_c, state_mlstm_n, state_mlstm_m)
    sample_ctx = (cache_k, cache_v, cache_idx_k, page_table)
    p_st, s_st = [], []
    for l in range(depth):
        xs, (p, s) = _trunk_layer(xs, l, P, rel_bias, sample_ctx, n_prompt, (bp, tp), (bs, ts), init_sample)
        p_st.append(p)
        s_st.append(s)
    y = _final_norm(xs, final_g[None])
    y_prompt = y[:n_prompt].reshape(bp, tp, d)
    y_sample = y[n_prompt:].reshape(bs, ts, d)

    def stacked(states, i):
        return jnp.stack([s[i] for s in states], axis=0)

    return (y_prompt, y_sample,
            *[stacked(p_st, i) for i in range(8)],
            *[stacked(s_st, i) for i in range(8)])
```

```python
import functools
import math

import jax
import jax.numpy as jnp
from jax import lax
from jax.experimental import pallas as pl
from jax.experimental.pallas import tpu as pltpu

F32 = jnp.float32
BF16 = jnp.bfloat16

HEAD_DIM = 128
CONV_W = 4
CHUNK = 64
IDX_HEADS = 16
IDX_DIM = 64
TOPK_MAX = 256
Q_BLOCK = 128
NUM_BUCKETS = 32
MAX_DISTANCE = 128
PAGE_SIZE = 128
N_BRANCH = 3
EPS = 1e-6

LANE = 128
VMEM_LIMIT = 56 * 1024 * 1024
DSA_TQ = 256


def _round_up(a, b):
    return (a + b - 1) // b * b


def _rms_rows(x, g):
    return x * lax.rsqrt(jnp.mean(x * x, axis=-1, keepdims=True) + EPS) * g


def _ffn_kernel(x_ref, g_ref, w1_ref, w3_ref, w2_ref, o_ref, n_ref):
    j = pl.program_id(1)

    @pl.when(j == 0)
    def _():
        n_ref[...] = _rms_rows(x_ref[...], g_ref[...]).astype(BF16)
        o_ref[...] = jnp.zeros_like(o_ref)

    n = n_ref[...]
    g = jnp.dot(n, w1_ref[...], preferred_element_type=F32)
    u = jnp.dot(n, w3_ref[...], preferred_element_type=F32)
    a = (g * jax.nn.sigmoid(g) * u).astype(BF16)
    o_ref[...] += jnp.dot(a, w2_ref[...], preferred_element_type=F32)

    @pl.when(j == pl.num_programs(1) - 1)
    def _():
        o_ref[...] = x_ref[...] + 0.5 * o_ref[...]


def _ffn(x, g, w13p, w2p, *, tm=512, tf=512):
    m, d = x.shape
    fp = w2p.shape[0]
    nj = fp // tf
    return pl.pallas_call(
        _ffn_kernel,
        grid=(m // tm, nj),
        in_specs=[
            pl.BlockSpec((tm, d), lambda i, j: (i, 0)),
            pl.BlockSpec((1, d), lambda i, j: (0, 0)),
            pl.BlockSpec((d, tf), lambda i, j: (0, j)),
            pl.BlockSpec((d, tf), lambda i, j: (0, j + nj)),
            pl.BlockSpec((tf, d), lambda i, j: (j, 0)),
        ],
        out_specs=pl.BlockSpec((tm, d), lambda i, j: (i, 0)),
        out_shape=jax.ShapeDtypeStruct((m, d), F32),
        scratch_shapes=[pltpu.VMEM((tm, d), BF16)],
        compiler_params=pltpu.CompilerParams(
            dimension_semantics=("parallel", "arbitrary"), vmem_limit_bytes=VMEM_LIMIT),
        name="ffn",
    )(x, g, w13p, w13p, w2p)


def _inproj_kernel(x_ref, g_ref, w_ref, b_ref, o_ref, k_ref, v_ref, n_ref, *, ck, cv, nh):
    j = pl.program_id(1)

    @pl.when(j == 0)
    def _():
        n_ref[...] = _rms_rows(x_ref[...], g_ref[...]).astype(BF16)

    res = jnp.dot(n_ref[...], w_ref[...], preferred_element_type=F32) + b_ref[...]
    o_ref[...] = res
    tm = res.shape[0]

    def heads_out(dst_ref):
        for h in range(nh):
            dst_ref[pl.ds(h, tm, stride=nh), :] = res[:, h * HEAD_DIM:(h + 1) * HEAD_DIM]

    @pl.when(j == ck)
    def _():
        heads_out(k_ref)

    @pl.when(j == cv)
    def _():
        heads_out(v_ref)


def _inproj(x, g, w, b, ck, cv, *, tm=512):
    m, d = x.shape
    n = w.shape[1]
    tn = d // 2
    nh = tn // HEAD_DIM
    kv_spec = pl.BlockSpec((tm * nh, HEAD_DIM), lambda i, j: (i, 0))
    kv_shape = jax.ShapeDtypeStruct((m * nh, HEAD_DIM), F32)
    return pl.pallas_call(
        functools.partial(_inproj_kernel, ck=ck, cv=cv, nh=nh),
        grid=(m // tm, n // tn),
        in_specs=[
            pl.BlockSpec((tm, d), lambda i, j: (i, 0)),
            pl.BlockSpec((1, d), lambda i, j: (0, 0)),
            pl.BlockSpec((d, tn), lambda i, j: (0, j)),
            pl.BlockSpec((1, tn), lambda i, j: (0, j)),
        ],
        out_specs=[pl.BlockSpec((tm, tn), lambda i, j: (i, j)), kv_spec, kv_spec],
        out_shape=[jax.ShapeDtypeStruct((m, n), F32), kv_shape, kv_shape],
        scratch_shapes=[pltpu.VMEM((tm, d), BF16)],
        compiler_params=pltpu.CompilerParams(
            dimension_semantics=("parallel", "arbitrary"), vmem_limit_bytes=VMEM_LIMIT),
        name="inproj",
    )(x, g, w, b)


def _inproj_small_kernel(x_ref, g_ref, w_ref, b_ref, lg_ref, lb_ref, o_ref, ki_ref):
    n = _rms_rows(x_ref[...], g_ref[...])
    o = jnp.dot(n, w_ref[...], preferred_element_type=F32, precision=lax.Precision.HIGHEST) + b_ref[...]
    o_ref[...] = o
    ik = o[:, :IDX_DIM]
    mu = jnp.mean(ik, axis=-1, keepdims=True)
    var = jnp.mean(jnp.square(ik - mu), axis=-1, keepdims=True)
    ki_ref[...] = (ik - mu) * lax.rsqrt(var + EPS) * lg_ref[...] + lb_ref[...]


def _inproj_small(x, g, w, b, ln_g, ln_b, *, tm=512):
    m, d = x.shape
    n = w.shape[1]
    return pl.pallas_call(
        _inproj_small_kernel,
        grid=(m // tm,),
        in_specs=[
            pl.BlockSpec((tm, d), lambda i: (i, 0)),
            pl.BlockSpec((1, d), lambda i: (0, 0)),
            pl.BlockSpec((d, n), lambda i: (0, 0)),
            pl.BlockSpec((1, n), lambda i: (0, 0)),
            pl.BlockSpec((1, IDX_DIM), lambda i: (0, 0)),
            pl.BlockSpec((1, IDX_DIM), lambda i: (0, 0)),
        ],
        out_specs=[pl.BlockSpec((tm, n), lambda i: (i, 0)), pl.BlockSpec((tm, IDX_DIM), lambda i: (i, 0))],
        out_shape=[jax.ShapeDtypeStruct((m, n), F32), jax.ShapeDtypeStruct((m, IDX_DIM), F32)],
        compiler_params=pltpu.CompilerParams(
            dimension_semantics=("parallel",), vmem_limit_bytes=VMEM_LIMIT),
        name="inproj_small",
    )(x, g, w, b, ln_g, ln_b)


def _merge_kernel(x_ref, g_ref, oa_ref, ob_ref, oc_ref, wg0_ref, wg1_ref, wg2_ref,
                  bg0_ref, bg1_ref, bg2_ref, wb0_ref, wb1_ref, wb2_ref, wo_ref, o_ref, n_ref):
    j = pl.program_id(1)

    @pl.when(j == 0)
    def _():
        n_ref[...] = _rms_rows(x_ref[...], g_ref[...]).astype(BF16)
        o_ref[...] = jnp.zeros_like(o_ref)

    n = n_ref[...]
    merged = None
    for o_r, wg_r, bg_r, wb_r in ((oa_ref, wg0_ref, bg0_ref, wb0_ref),
                                  (ob_ref, wg1_ref, bg1_ref, wb1_ref),
                                  (oc_ref, wg2_ref, bg2_ref, wb2_ref)):
        gate = jax.nn.sigmoid(jnp.dot(n, wg_r[...], preferred_element_type=F32) + bg_r[...])
        y = jnp.dot(o_r[...], wb_r[0], preferred_element_type=F32)
        merged = gate * y if merged is None else merged + gate * y
    o_ref[...] += jnp.dot(merged.astype(BF16), wo_ref[...], preferred_element_type=F32)

    @pl.when(j == pl.num_programs(1) - 1)
    def _():
        o_ref[...] = x_ref[...] + o_ref[...]


def _merge(x, g, oa, ob, oc, wg, bg, wb, wo, *, tm=512, tn=512):
    m, d = x.shape
    w = oa.shape[1]
    nj = d // tn
    wg_specs = [pl.BlockSpec((d, tn), functools.partial(lambda i, j, n: (0, n * nj + j), n=n))
                for n in range(N_BRANCH)]
    bg_specs = [pl.BlockSpec((1, tn), functools.partial(lambda i, j, n: (0, n * nj + j), n=n))
                for n in range(N_BRANCH)]
    wb_specs = [pl.BlockSpec((1, w, tn), functools.partial(lambda i, j, n: (n, 0, j), n=n))
                for n in range(N_BRANCH)]
    o_spec = pl.BlockSpec((tm, w), lambda i, j: (i, 0))
    return pl.pallas_call(
        _merge_kernel,
        grid=(m // tm, nj),
        in_specs=[pl.BlockSpec((tm, d), lambda i, j: (i, 0)),
                  pl.BlockSpec((1, d), lambda i, j: (0, 0)),
                  o_spec, o_spec, o_spec,
                  *wg_specs, *bg_specs, *wb_specs,
                  pl.BlockSpec((tn, d), lambda i, j: (j, 0))],
        out_specs=pl.BlockSpec((tm, d), lambda i, j: (i, 0)),
        out_shape=jax.ShapeDtypeStruct((m, d), F32),
        scratch_shapes=[pltpu.VMEM((tm, d), BF16)],
        compiler_params=pltpu.CompilerParams(
            dimension_semantics=("parallel", "arbitrary"), vmem_limit_bytes=VMEM_LIMIT),
        name="merge",
    )(x, g, oa, ob, oc, wg, wg, wg, bg, bg, bg, wb, wb, wb, wo)


def _norm_kernel(x_ref, g_ref, o_ref):
    o_ref[...] = _rms_rows(x_ref[...], g_ref[...])


def _final_norm(x, g, *, tm=512):
    m, d = x.shape
    return pl.pallas_call(
        _norm_kernel,
        grid=(m // tm,),
        in_specs=[pl.BlockSpec((tm, d), lambda i: (i, 0)), pl.BlockSpec((1, d), lambda i: (0, 0))],
        out_specs=pl.BlockSpec((tm, d), lambda i: (i, 0)),
        out_shape=jax.ShapeDtypeStruct((m, d), F32),
        compiler_params=pltpu.CompilerParams(dimension_semantics=("parallel",)),
        name="final_norm",
    )(x, g)


NEG = -1e30
LOG2E = math.log2(math.e)
INT_MIN = -2 ** 31


def _order_key(x):
    bits = lax.bitcast_convert_type(x, jnp.int32)
    return bits ^ ((bits >> 31) & jnp.int32(0x7FFFFFFF))


def _kth_largest_key(count_ge, topk, shape):
    t0 = jnp.where(count_ge(jnp.zeros(shape, jnp.int32)) >= topk, jnp.int32(0), jnp.int32(INT_MIN))

    def bit(b, t):
        cand = t + lax.shift_left(jnp.int32(1), jnp.int32(30) - b)
        return jnp.where(count_ge(cand) >= topk, cand, t)

    return lax.fori_loop(0, 31, bit, t0)


def _kth_largest_key_radix(count_ge, topk, shape, bits):
    t = jnp.where(count_ge(jnp.zeros(shape, jnp.int32)) >= topk, jnp.int32(0), jnp.int32(INT_MIN))
    hi = 31
    while hi > 0:
        nb = hi % bits or bits
        sh = hi - nb
        digit = jnp.zeros(shape, jnp.int32)
        for d in range(1, 2 ** nb):
            digit = digit + jnp.where(count_ge(t + jnp.int32(d << sh)) >= topk, 1, 0)
        t = t + lax.shift_left(digit, jnp.int32(sh))
        hi = sh
    return t


def _softmax_update(s, vb, m_prev, l_prev, acc_prev):
    reps = s.shape[1] // LANE
    m_new = jnp.maximum(m_prev, jnp.max(s, axis=1, keepdims=True))
    alpha = jnp.exp2(m_prev - m_new)
    p = jnp.exp2(s - jnp.concatenate([m_new] * reps, axis=1))
    l_new = alpha * l_prev + jnp.sum(p, axis=1, keepdims=True)
    pv = jnp.dot(p.astype(BF16), vb, preferred_element_type=F32)
    dreps = pv.shape[1] // LANE
    acc_new = jnp.concatenate([alpha] * dreps, axis=1) * acc_prev + pv
    return m_new, l_new, acc_new


def _dsa_prompt_kernel(it_ref, jt_ref, qi_ref, w_ref, ki_ref, q_ref, k_ref, v_ref, bd_ref, bs_ref,
                       o_ref, key_ref, thr_ref, wb_ref, qb_ref, m_ref, l_ref, acc_ref, *, topk, tq, nh):
    i = it_ref[pl.program_id(0)]
    j = jt_ref[pl.program_id(0)]
    rg = 128

    @pl.when(j == 0)
    def _select():
        for h in range(IDX_HEADS):
            wb_ref[h] = jnp.broadcast_to(w_ref[:, h:h + 1], (tq, tq))
        qb_ref[...] = (q_ref[...] * (HEAD_DIM ** -0.5 * LOG2E)).astype(BF16)
        m_ref[...] = jnp.full_like(m_ref, NEG)
        l_ref[...] = jnp.zeros_like(l_ref)
        acc_ref[...] = jnp.zeros_like(acc_ref)
        qi = _heads_to_rows(qi_ref)
        row = lax.broadcasted_iota(jnp.int32, (tq, tq), 0)
        col = lax.broadcasted_iota(jnp.int32, (tq, tq), 1)

        def chunk(c, carry):
            kc = ki_ref[pl.ds(pl.multiple_of(c * tq, tq), tq), :]
            s = lax.dot_general(qi, kc, (((1,), (1,)), ((), ())), preferred_element_type=F32)
            sc = jnp.zeros((tq, tq), F32)
            for h in range(IDX_HEADS):
                sc = sc + jnp.maximum(s[h * tq:(h + 1) * tq], 0.0) * wb_ref[h]
            sc = jnp.where(col <= row + jnp.where(c < i, tq, 0), sc, -jnp.inf)
            key_ref[c] = _order_key(sc)
            return carry

        lax.fori_loop(0, i + 1, chunk, 0)

        for r in range(tq // rg):
            rows = pl.ds(r * rg, rg)

            def count_ge(cand):
                def body(c, a):
                    hit = jnp.where(key_ref[c, rows, :] >= cand, 1.0, 0.0)
                    for x in range(tq // LANE):
                        a = a + hit[:, x * LANE:(x + 1) * LANE]
                    return a
                a = lax.fori_loop(0, i + 1, body, jnp.zeros((rg, LANE), F32))
                return jnp.sum(a, axis=1, keepdims=True)

            t = _kth_largest_key(count_ge, float(topk), (rg, 1))
            thr_ref[rows, :] = jnp.broadcast_to(t, (rg, LANE))

    def attend(bias_of_head):
        kk = key_ref[j]
        thr = thr_ref[...]
        selb = jnp.concatenate(
            [jnp.where(kk[:, x * LANE:(x + 1) * LANE] >= thr, 0.0, NEG) for x in range(tq // LANE)], axis=1)
        kb = k_ref[...].astype(BF16)
        vb = v_ref[...].astype(BF16)
        for h in range(nh):
            hs = slice(h * HEAD_DIM, (h + 1) * HEAD_DIM)
            s = lax.dot_general(qb_ref[:, hs], kb[:, hs], (((1,), (1,)), ((), ())),
                                preferred_element_type=F32)
            s = s + selb if bias_of_head is None else s + selb + bias_of_head(h)
            m_new, l_new, acc_new = _softmax_update(s, vb[:, hs], m_ref[h], l_ref[h], acc_ref[:, hs])
            m_ref[h] = m_new
            l_ref[h] = l_new
            acc_ref[:, hs] = acc_new

    @pl.when(j < i - 1)
    def _far():
        attend(None)

    @pl.when(j == i - 1)
    def _sub():
        attend(lambda h: bs_ref[h])

    @pl.when(j == i)
    def _diag():
        attend(lambda h: bd_ref[h])
        for h in range(nh):
            hs = slice(h * HEAD_DIM, (h + 1) * HEAD_DIM)
            o_ref[:, hs] = (acc_ref[:, hs] / l_ref[h]).astype(BF16)


def _t5_bucket(dist):
    n = jnp.maximum(dist, 0)
    max_exact = NUM_BUCKETS // 2
    nf = jnp.maximum(n, 1).astype(F32)
    large = max_exact + (jnp.log(nf / max_exact) / math.log(MAX_DISTANCE / max_exact)
                         * (NUM_BUCKETS - max_exact)).astype(jnp.int32)
    large = jnp.minimum(large, NUM_BUCKETS - 1)
    return jnp.where(n < max_exact, n, large)


def _t5_bias(rel_bias, dist):
    onehot = jax.nn.one_hot(_t5_bucket(dist), NUM_BUCKETS, dtype=F32)
    return jnp.einsum('...b,bh->h...', onehot, rel_bias, precision=lax.Precision.HIGHEST)


def _t5_prompt_tiles(rel_bias, tq):
    assert tq >= MAX_DISTANCE
    r = jnp.arange(tq)[:, None]
    c = jnp.arange(tq)[None, :]
    far = _t5_bias(rel_bias, jnp.int32(tq + 1))[:, None, None]
    bd = jnp.where((r >= c)[None], (_t5_bias(rel_bias, r - c) - far) * LOG2E, NEG)
    bs = (_t5_bias(rel_bias, tq + r - c) - far) * LOG2E
    return bd, bs


def _heads_to_rows(qi_ref):
    return jnp.concatenate([qi_ref[:, h * IDX_DIM:(h + 1) * IDX_DIM] for h in range(IDX_HEADS)],
                           axis=0).astype(BF16)


def _dsa_prompt(big, row0, t, cols, w, ki, rel_bias, tiles=None, *, tq=DSA_TQ):
    bd, bs = _t5_prompt_tiles(rel_bias, tq) if tiles is None else tiles
    nh = rel_bias.shape[1]
    mw = nh * HEAD_DIM
    assert t % tq == 0 and row0 % tq == 0
    nq = t // tq
    rb = row0 // tq
    topk = min(TOPK_MAX, t // 4)
    pairs = [(i, j) for i in range(nq) for j in range(i + 1)]
    it = jnp.asarray([p[0] for p in pairs], jnp.int32)
    jt = jnp.asarray([p[1] for p in pairs], jnp.int32)
    cq, ck, cv, ci = cols
    kern = functools.partial(_dsa_prompt_kernel, topk=topk, tq=tq, nh=nh)
    grid_spec = pltpu.PrefetchScalarGridSpec(
        num_scalar_prefetch=2,
        grid=(len(pairs),),
        in_specs=[
            pl.BlockSpec((tq, IDX_HEADS * IDX_DIM), lambda s, it, jt: (rb + it[s], ci)),
            pl.BlockSpec((tq, IDX_HEADS), lambda s, it, jt: (it[s], 0)),
            pl.BlockSpec((t, IDX_DIM), lambda s, it, jt: (0, 0)),
            pl.BlockSpec((tq, mw), lambda s, it, jt: (rb + it[s], cq)),
            pl.BlockSpec((tq, mw), lambda s, it, jt: (rb + jt[s], ck)),
            pl.BlockSpec((tq, mw), lambda s, it, jt: (rb + jt[s], cv)),
            pl.BlockSpec((nh, tq, tq), lambda s, it, jt: (0, 0, 0)),
            pl.BlockSpec((nh, tq, tq), lambda s, it, jt: (0, 0, 0)),
        ],
        out_specs=pl.BlockSpec((tq, mw), lambda s, it, jt: (it[s], 0)),
        scratch_shapes=[
            pltpu.VMEM((nq, tq, tq), jnp.int32),
            pltpu.VMEM((tq, LANE), jnp.int32),
            pltpu.VMEM((IDX_HEADS, tq, tq), F32),
            pltpu.VMEM((tq, mw), BF16),
            pltpu.VMEM((nh, tq, LANE), F32),
            pltpu.VMEM((nh, tq, LANE), F32),
            pltpu.VMEM((tq, mw), F32),
        ],
    )
    return pl.pallas_call(
        kern,
        grid_spec=grid_spec,
        out_shape=jax.ShapeDtypeStruct((t, mw), BF16),
        compiler_params=pltpu.CompilerParams(
            dimension_semantics=("arbitrary",), vmem_limit_bytes=VMEM_LIMIT),
        name="dsa_prompt",
    )(it, jt, big, w, ki, big, big, big, bd, bs)


def _dsa_sample_kernel(pt_ref, qi_ref, w_ref, kin_ref, q_ref, kn_ref, vn_ref, bias_ref, biasn_ref, *rest,
                       topk, t, n_pages, g_pages, nh, past):
    idx_refs = rest[:n_pages]
    kp_refs = rest[n_pages:n_pages + g_pages]
    vp_refs = rest[n_pages + g_pages:n_pages + 2 * g_pages]
    o_ref, selb_ref, selbn_ref, qbd_ref, s_ref, vb_ref = rest[n_pages + 2 * g_pages:]
    g = pl.program_id(1)
    ng = n_pages // g_pages
    gw = g_pages * PAGE_SIZE
    rows = nh * t
    mw = nh * HEAD_DIM

    @pl.when(g == 0)
    def _select():
        qi = _heads_to_rows(qi_ref)
        wcol = w_ref[...]
        pad = jnp.zeros((PAGE_SIZE - t, IDX_DIM), F32)
        keys = jnp.concatenate([r[...] for r in idx_refs] + [kin_ref[...], pad], axis=0).astype(BF16)
        s = lax.dot_general(qi, keys, (((1,), (1,)), ((), ())), preferred_element_type=F32)
        s = jnp.maximum(s, 0.0) * wcol
        sc = s[0:t]
        for h in range(1, IDX_HEADS):
            sc = sc + s[h * t:(h + 1) * t]
        kpos = lax.broadcasted_iota(jnp.int32, sc.shape, 1)
        qrow = lax.broadcasted_iota(jnp.int32, sc.shape, 0)
        key = _order_key(jnp.where(kpos <= past + qrow, sc, -jnp.inf))

        def count_ge(cand):
            return jnp.sum(jnp.where(key >= cand, 1.0, 0.0), axis=1, keepdims=True)

        thr = _kth_largest_key_radix(count_ge, float(topk), (t, 1), 4)
        selb = jnp.where(key >= thr, 0.0, NEG)
        for x in range(ng):
            selb_ref[x] = selb[:, x * gw:(x + 1) * gw]
        selbn_ref[...] = selb[:, past:]

        qs = jnp.concatenate([q_ref[...] * (HEAD_DIM ** -0.5 * LOG2E)] * nh, axis=0)
        rr = lax.broadcasted_iota(jnp.int32, (rows, mw), 0) >> (t.bit_length() - 1)
        cc = lax.broadcasted_iota(jnp.int32, (rows, mw), 1) >> (HEAD_DIM.bit_length() - 1)
        qbd_ref[...] = jnp.where(rr == cc, qs, 0.0).astype(BF16)

    def logits(kb, bias):
        return lax.dot_general(qbd_ref[...], kb, (((1,), (1,)), ((), ())), preferred_element_type=F32) + bias

    def page(r):
        return jnp.concatenate([r[pl.ds(h, PAGE_SIZE, stride=nh), :] for h in range(nh)], axis=1)

    for x in range(g_pages):
        ps = slice(x * PAGE_SIZE, (x + 1) * PAGE_SIZE)
        bias = bias_ref[g, :, ps] + jnp.concatenate([selb_ref[g, :, ps]] * nh, axis=0)
        s_ref[g, :, ps] = logits(page(kp_refs[x]).astype(BF16), bias)
        vb_ref[g, ps, :] = page(vp_refs[x]).astype(BF16)

    @pl.when(g == ng - 1)
    def _finish():
        padn = jnp.zeros((PAGE_SIZE - t, mw), F32)
        kn = jnp.concatenate([kn_ref[...], padn], axis=0).astype(BF16)
        vn = jnp.concatenate([vn_ref[...], padn], axis=0).astype(BF16)
        s_new = logits(kn, biasn_ref[...] + jnp.concatenate([selbn_ref[...]] * nh, axis=0))
        s_past = [s_ref[x] for x in range(ng)]
        m = jnp.max(s_new, axis=1, keepdims=True)
        for x in range(ng):
            m = jnp.maximum(m, jnp.max(s_past[x], axis=1, keepdims=True))
        p_new = jnp.exp2(s_new - m)
        den = jnp.sum(p_new, axis=1, keepdims=True)
        acc = jnp.dot(p_new.astype(BF16), vn, preferred_element_type=F32)
        for x in range(ng):
            p = jnp.exp2(s_past[x] - m)
            den = den + jnp.sum(p, axis=1, keepdims=True)
            acc = acc + jnp.dot(p.astype(BF16), vb_ref[x], preferred_element_type=F32)
        for h in range(nh):
            hs = slice(h * HEAD_DIM, (h + 1) * HEAD_DIM)
            rs = slice(h * t, (h + 1) * t)
            o_ref[:, hs] = (acc[rs, hs] / den[rs, :]).astype(BF16)


def _dsa_sample(big, row0, b, t, cols, w, kin, cache_k, cache_v, cache_idx_k, l, page_table, rel_bias):
    nh = rel_bias.shape[1]
    mw = nh * HEAD_DIM
    n_pages = page_table.shape[1]
    past = n_pages * PAGE_SIZE
    g_pages = math.gcd(n_pages, 4)
    ng = n_pages // g_pages
    gw = g_pages * PAGE_SIZE
    topk = min(TOPK_MAX, (past + t) // 4)
    assert row0 % t == 0 and t % 8 == 0 and t <= PAGE_SIZE and t & (t - 1) == 0
    rb = row0 // t
    rows = nh * t
    qpos = past + jnp.arange(t)
    kpos = jnp.arange(past + PAGE_SIZE)
    dist = qpos[:, None] - kpos[None, :]
    bias = jnp.where((dist >= 0)[None], _t5_bias(rel_bias, dist) * LOG2E, NEG)
    bias = bias.reshape(rows, past + PAGE_SIZE)
    bias_past = bias[:, :past].reshape(rows, ng, gw).transpose(1, 0, 2)
    bias_new = bias[:, past:]
    cq, ck, cv, ci = cols

    def page_map(x, bi, gi, pt):
        return (l, pt[bi, gi * g_pages + x], 0, 0)

    def idx_map(x, bi, gi, pt):
        return (l, pt[bi, x], 0, 0)

    in_specs = [
        pl.BlockSpec((t, IDX_HEADS * IDX_DIM), lambda bi, gi, pt: (rb + bi, ci)),
        pl.BlockSpec((None, IDX_HEADS * t, 1), lambda bi, gi, pt: (bi, 0, 0)),
        pl.BlockSpec((None, t, IDX_DIM), lambda bi, gi, pt: (bi, 0, 0)),
        pl.BlockSpec((t, mw), lambda bi, gi, pt: (rb + bi, cq)),
        pl.BlockSpec((t, mw), lambda bi, gi, pt: (rb + bi, ck)),
        pl.BlockSpec((t, mw), lambda bi, gi, pt: (rb + bi, cv)),
        pl.BlockSpec((ng, rows, gw), lambda bi, gi, pt: (0, 0, 0)),
        pl.BlockSpec((rows, PAGE_SIZE), lambda bi, gi, pt: (0, 0)),
    ]
    in_specs += [pl.BlockSpec((None, None, PAGE_SIZE, IDX_DIM), functools.partial(idx_map, x))
                 for x in range(n_pages)]
    in_specs += [pl.BlockSpec((None, None, PAGE_SIZE * nh, HEAD_DIM), functools.partial(page_map, x % g_pages))
                 for x in range(2 * g_pages)]
    kern = functools.partial(_dsa_sample_kernel, topk=topk, t=t, n_pages=n_pages, g_pages=g_pages,
                             nh=nh, past=past)
    grid_spec = pltpu.PrefetchScalarGridSpec(
        num_scalar_prefetch=1,
        grid=(b, ng),
        in_specs=in_specs,
        out_specs=pl.BlockSpec((t, mw), lambda bi, gi, pt: (bi, 0)),
        scratch_shapes=[
            pltpu.VMEM((ng, t, gw), F32),
            pltpu.VMEM((t, PAGE_SIZE), F32),
            pltpu.VMEM((rows, mw), BF16),
            pltpu.VMEM((ng, rows, gw), F32),
            pltpu.VMEM((ng, gw, mw), BF16),
        ],
    )
    return pl.pallas_call(
        kern,
        grid_spec=grid_spec,
        out_shape=jax.ShapeDtypeStruct((b * t, mw), BF16),
        compiler_params=pltpu.CompilerParams(
            dimension_semantics=("parallel", "arbitrary"), vmem_limit_bytes=VMEM_LIMIT),
        name="dsa_sample",
    )(page_table, big, w, kin, big, big, big, bias_past, bias_new,
      *([cache_idx_k] * n_pages), *([cache_k] * g_pages), *([cache_v] * g_pages))


def _lane_expand(x, nh):
    c = x.shape[0]
    return jnp.concatenate([jnp.broadcast_to(x[:, h:h + 1], (c, HEAD_DIM)) for h in range(nh)], axis=1)


def _bdot(a, b):
    return jnp.dot(a.astype(BF16), b.astype(BF16), preferred_element_type=F32)


def _dot01(a01, b):
    a = a01.astype(BF16)
    out = None
    rest = b
    for _ in range(3):
        part = rest.astype(BF16)
        rest = rest - part.astype(F32)
        term = jnp.dot(a, part, preferred_element_type=F32)
        out = term if out is None else out + term
    return out


def _bdot_nt(a, b):
    return lax.dot_general(a.astype(BF16), b.astype(BF16), (((1,), (1,)), ((), ())), preferred_element_type=F32)


def _bdot_tn(a, b):
    return lax.dot_general(a.astype(BF16), b.astype(BF16), (((0,), (0,)), ((), ())), preferred_element_type=F32)


def _softplus(x):
    return jnp.maximum(x, 0.0) + jnp.log(1.0 + jnp.exp(-jnp.abs(x)))


def _gdn_kernel(qkv_ref, z_ref, sm_ref, cw_ref, alog_ref, dtb_ref, ng_ref, s0_ref, conv0_ref,
                o_ref, s_out_ref, conv_out_ref, s_ref, xp_ref, *, c, nh, ob, oa):
    n = pl.program_id(1)
    mw = nh * HEAD_DIM
    halo = 8

    @pl.when(n == 0)
    def _():
        s_ref[...] = s0_ref[...]
        xp_ref[0:halo, :] = conv0_ref[...]

    xp_ref[halo:halo + c, :] = qkv_ref[...]
    y = cw_ref[0:1, :] * xp_ref[halo - CONV_W + 1:halo - CONV_W + 1 + c, :]
    for j in range(1, CONV_W):
        y = y + cw_ref[j:j + 1, :] * xp_ref[halo - CONV_W + 1 + j:halo - CONV_W + 1 + j + c, :]
    act = y * jax.nn.sigmoid(y)

    sm = sm_ref[...]
    beta = jax.nn.sigmoid(sm[:, ob:ob + nh])
    g = -jnp.exp(alog_ref[...]) * _softplus(sm[:, oa:oa + nh] + dtb_ref[...])
    beta_e = _lane_expand(beta, nh)
    g_e = _lane_expand(g, nh)
    t_i = lax.broadcasted_iota(jnp.int32, (c, mw), 0)
    j_i = lax.broadcasted_iota(jnp.int32, (c, mw), 1) & (HEAD_DIM - 1)
    ii = lax.broadcasted_iota(jnp.int32, (c, c), 0)
    jj = lax.broadcasted_iota(jnp.int32, (c, c), 1)
    tri = jnp.where(jj <= ii, 1.0, 0.0)
    cum = _dot01(tri, jnp.concatenate([jnp.where(t_i > j_i, g_e, 0.0), g, jnp.zeros((c, LANE - nh), F32)], axis=1))
    dmat = cum[:, :mw]
    gc = _lane_expand(cum[:, mw:mw + nh], nh)
    g_end = gc[c - 1:c, :]
    e_gc = jnp.exp(gc)
    e_rest = jnp.exp(g_end - gc)
    e_end = jnp.exp(g_end)
    eye = jnp.where(ii == jj, 1.0, 0.0)

    heads = range(nh)
    hsl = [slice(h * HEAD_DIM, (h + 1) * HEAD_DIM) for h in heads]
    qn, kn, kb, dec = [], [], [], []
    for h in heads:
        qh = act[:, h * HEAD_DIM:(h + 1) * HEAD_DIM]
        kh = act[:, mw + h * HEAD_DIM:mw + (h + 1) * HEAD_DIM]
        qn.append(qh * lax.rsqrt(jnp.sum(qh * qh, axis=1, keepdims=True) + EPS) * HEAD_DIM ** -0.5)
        kn.append(kh * lax.rsqrt(jnp.sum(kh * kh, axis=1, keepdims=True) + EPS))
        kb.append(kn[h] * beta_e[:, hsl[h]])
        dec.append(jnp.where(ii >= jj, jnp.exp(dmat[:, h * HEAD_DIM:h * HEAD_DIM + c]), 0.0))
    knb = [kn[h].astype(BF16) for h in heads]
    kk = [_bdot_nt(kb[h], knb[h]) for h in heads]
    qk = [_bdot_nt(qn[h], knb[h]) for h in heads]
    x = [jnp.where(ii > jj, -kk[h] * dec[h], 0.0) for h in heads]
    p = [eye + x[h] for h in heads]
    for _ in range(c.bit_length() - 2):
        x = [_bdot(x[h], x[h]) for h in heads]
        p = [p[h] + _bdot(p[h], x[h]) for h in heads]
    uw = [_bdot(p[h], jnp.concatenate(
        [act[:, 2 * mw + h * HEAD_DIM:2 * mw + (h + 1) * HEAD_DIM] * beta_e[:, hsl[h]],
         kb[h] * e_gc[:, hsl[h]]], axis=1)) for h in heads]
    s_old = [s_ref[h] for h in heads]
    sb = [s_old[h].astype(BF16) for h in heads]
    o_inter = [_bdot(qn[h] * e_gc[:, hsl[h]], sb[h]) for h in heads]
    v_new = [uw[h][:, :HEAD_DIM] - _bdot(uw[h][:, HEAD_DIM:], sb[h]) for h in heads]
    o_intra = [_bdot(qk[h] * dec[h], v_new[h]) for h in heads]
    kv = [_bdot_tn(kn[h] * e_rest[:, hsl[h]], v_new[h]) for h in heads]
    for h in heads:
        s_ref[h] = s_old[h] * e_end[:, hsl[h]] + kv[h]
        o = o_inter[h] + o_intra[h]
        on = o * lax.rsqrt(jnp.mean(o * o, axis=1, keepdims=True) + EPS) * ng_ref[...]
        zh = z_ref[:, hsl[h]]
        o_ref[:, hsl[h]] = (on * zh * jax.nn.sigmoid(zh)).astype(o_ref.dtype)

    tail = xp_ref[c:c + halo, :]
    xp_ref[0:halo, :] = tail

    @pl.when(n == pl.num_programs(1) - 1)
    def _():
        s_out_ref[...] = s_ref[...]
        conv_out_ref[...] = tail


def _gdn(big, small, row0, b, t, cq, cz, ob, oa, conv_w, a_log, dt_bias, norm_g, s0, conv0):
    nh = s0.shape[1]
    mw = nh * HEAD_DIM
    c = math.gcd(t, CHUNK)
    assert c % 8 == 0 and c & (c - 1) == 0 and row0 % c == 0
    nc = t // c
    rb = row0 // c
    halo = 8
    conv0p = jnp.pad(conv0, ((0, 0), (halo - (CONV_W - 1), 0), (0, 0)))
    odt = BF16 if c % 16 == 0 else F32
    kern = functools.partial(_gdn_kernel, c=c, nh=nh, ob=ob, oa=oa)
    o, s_new, conv_new = pl.pallas_call(
        kern,
        grid=(b, nc),
        in_specs=[
            pl.BlockSpec((c, 3 * mw), lambda bi, n: (rb + bi * nc + n, cq)),
            pl.BlockSpec((c, mw), lambda bi, n: (rb + bi * nc + n, cz)),
            pl.BlockSpec((c, LANE), lambda bi, n: (rb + bi * nc + n, 0)),
            pl.BlockSpec((CONV_W, 3 * mw), lambda bi, n: (0, 0)),
            pl.BlockSpec((1, nh), lambda bi, n: (0, 0)),
            pl.BlockSpec((1, nh), lambda bi, n: (0, 0)),
            pl.BlockSpec((1, HEAD_DIM), lambda bi, n: (0, 0)),
            pl.BlockSpec((None, nh, HEAD_DIM, HEAD_DIM), lambda bi, n: (bi, 0, 0, 0)),
            pl.BlockSpec((None, halo, 3 * mw), lambda bi, n: (bi, 0, 0)),
        ],
        out_specs=[
            pl.BlockSpec((c, mw), lambda bi, n: (bi * nc + n, 0)),
            pl.BlockSpec((None, nh, HEAD_DIM, HEAD_DIM), lambda bi, n: (bi, 0, 0, 0)),
            pl.BlockSpec((None, halo, 3 * mw), lambda bi, n: (bi, 0, 0)),
        ],
        out_shape=[
            jax.ShapeDtypeStruct((b * t, mw), odt),
            jax.ShapeDtypeStruct((b, nh, HEAD_DIM, HEAD_DIM), F32),
            jax.ShapeDtypeStruct((b, halo, 3 * mw), F32),
        ],
        scratch_shapes=[pltpu.VMEM((nh, HEAD_DIM, HEAD_DIM), F32), pltpu.VMEM((halo + c, 3 * mw), F32)],
        compiler_params=pltpu.CompilerParams(
            dimension_semantics=("parallel", "arbitrary"), vmem_limit_bytes=VMEM_LIMIT),
        name="gdn",
    )(big, big, small, conv_w, a_log[None], dt_bias[None], norm_g[None], s0, conv0p)
    return o, s_new, conv_new[:, halo - (CONV_W - 1):]


def _mlstm_kernel(q_ref, k_ref, v_ref, og_ref, sm_ref, ng_ref, c0_ref, n0_ref, m0_ref,
                  o_ref, c_out_ref, n_out_ref, m_out_ref, c_ref, n_ref, m_ref, *, c, nh, oi, of):
    n = pl.program_id(1)
    mw = nh * HEAD_DIM

    @pl.when(n == 0)
    def _():
        c_ref[...] = c0_ref[...]
        n_ref[...] = n0_ref[...]
        m_ref[...] = m0_ref[...]

    sm = sm_ref[...]
    ig_e = _lane_expand(sm[:, oi:oi + nh], nh)
    lf_e = _lane_expand(-_softplus(-sm[:, of:of + nh]), nh)
    t_i = lax.broadcasted_iota(jnp.int32, (c, mw), 0)
    j_i = lax.broadcasted_iota(jnp.int32, (c, mw), 1) & (HEAD_DIM - 1)
    ii = lax.broadcasted_iota(jnp.int32, (c, c), 0)
    jj = lax.broadcasted_iota(jnp.int32, (c, c), 1)
    lhs = jnp.concatenate([jnp.where(jj <= ii, 1.0, 0.0), jnp.ones((c, c), F32)], axis=1)
    rhs = jnp.concatenate([
        jnp.concatenate([lf_e, jnp.where(t_i > j_i, lf_e, 0.0)], axis=1),
        jnp.concatenate([jnp.zeros((c, mw), F32), jnp.where(t_i == j_i, ig_e, 0.0)], axis=1)], axis=0)
    cum = jnp.dot(lhs, rhs, preferred_element_type=F32, precision=lax.Precision.HIGHEST)
    bc = cum[:, :mw]
    dlog = cum[:, mw:]
    b_end = bc[c - 1:c, :]
    a_end = b_end - bc + ig_e

    heads = range(nh)
    hsl = [slice(h * HEAD_DIM, (h + 1) * HEAD_DIM) for h in heads]
    qb = [(q_ref[:, hsl[h]] * HEAD_DIM ** -0.5).astype(BF16) for h in heads]
    kh = [k_ref[:, hsl[h]] for h in heads]
    vb = [v_ref[:, hsl[h]].astype(BF16) for h in heads]
    cs = [c_ref[h] for h in heads]
    qk = [_bdot_nt(qb[h], kh[h]) for h in heads]
    qc = [_bdot(qb[h], cs[h]) for h in heads]
    m_prev = [m_ref[h:h + 1, :] for h in heads]
    m_new = [jnp.maximum(b_end[:, hsl[h]] + m_prev[h], jnp.max(a_end[:, hsl[h]], axis=0, keepdims=True))
             for h in heads]
    kw = [kh[h] * jnp.exp(a_end[:, hsl[h]] - m_new[h]) for h in heads]
    kv = [_bdot_tn(kw[h], vb[h]) for h in heads]
    mt, sc = [], []
    for h in heads:
        dl = jnp.where(ii >= jj, dlog[:, h * HEAD_DIM:h * HEAD_DIM + c], -jnp.inf)
        inter = bc[:, hsl[h]] + m_prev[h]
        mt.append(jnp.maximum(inter, jnp.max(dl, axis=1, keepdims=True)))
        sc.append(qk[h] * jnp.exp(dl - mt[h][:, :c]))
    sv = [_bdot(sc[h], vb[h]) for h in heads]
    for h in heads:
        w_inter = jnp.exp(bc[:, hsl[h]] + m_prev[h] - mt[h])
        ns = n_ref[h:h + 1, :]
        qf = q_ref[:, hsl[h]] * HEAD_DIM ** -0.5
        num = w_inter * qc[h] + sv[h]
        den = w_inter * jnp.sum(qf * ns, axis=1, keepdims=True) + jnp.sum(sc[h], axis=1, keepdims=True)
        hh = num / jnp.maximum(jnp.abs(den), jnp.exp(-mt[h]))
        keep = jnp.exp(b_end[:, hsl[h]] + m_prev[h] - m_new[h])
        c_ref[h] = keep * cs[h] + kv[h]
        n_ref[h:h + 1, :] = keep * ns + jnp.sum(kw[h], axis=0, keepdims=True)
        m_ref[h:h + 1, :] = m_new[h]
        hn = hh * lax.rsqrt(jnp.mean(hh * hh, axis=1, keepdims=True) + EPS) * ng_ref[...]
        o_ref[:, hsl[h]] = (hn * jax.nn.sigmoid(og_ref[:, hsl[h]])).astype(o_ref.dtype)

    @pl.when(n == pl.num_programs(1) - 1)
    def _():
        c_out_ref[...] = c_ref[...]
        n_out_ref[...] = n_ref[...]
        m_out_ref[...] = m_ref[...]


def _mlstm(big, small, row0, b, t, cols, oi, of, norm_g, c0, n0, m0):
    nh = c0.shape[1]
    mw = nh * HEAD_DIM
    c = math.gcd(t, CHUNK)
    assert c % 8 == 0 and c <= HEAD_DIM and row0 % c == 0
    nc = t // c
    rb = row0 // c
    odt = BF16 if c % 16 == 0 else F32
    m0e = jnp.broadcast_to(m0[:, :, None], (b, nh, LANE))

    def col_spec(cc):
        return pl.BlockSpec((c, mw), lambda bi, n: (rb + bi * nc + n, cc))

    kern = functools.partial(_mlstm_kernel, c=c, nh=nh, oi=oi, of=of)
    o, c_new, n_new, m_new = pl.pallas_call(
        kern,
        grid=(b, nc),
        in_specs=[
            *[col_spec(cc) for cc in cols],
            pl.BlockSpec((c, LANE), lambda bi, n: (rb + bi * nc + n, 0)),
            pl.BlockSpec((1, HEAD_DIM), lambda bi, n: (0, 0)),
            pl.BlockSpec((None, nh, HEAD_DIM, HEAD_DIM), lambda bi, n: (bi, 0, 0, 0)),
            pl.BlockSpec((None, nh, HEAD_DIM), lambda bi, n: (bi, 0, 0)),
            pl.BlockSpec((None, nh, LANE), lambda bi, n: (bi, 0, 0)),
        ],
        out_specs=[
            pl.BlockSpec((c, mw), lambda bi, n: (bi * nc + n, 0)),
            pl.BlockSpec((None, nh, HEAD_DIM, HEAD_DIM), lambda bi, n: (bi, 0, 0, 0)),
            pl.BlockSpec((None, nh, HEAD_DIM), lambda bi, n: (bi, 0, 0)),
            pl.BlockSpec((None, nh, LANE), lambda bi, n: (bi, 0, 0)),
        ],
        out_shape=[
            jax.ShapeDtypeStruct((b * t, mw), odt),
            jax.ShapeDtypeStruct((b, nh, HEAD_DIM, HEAD_DIM), F32),
            jax.ShapeDtypeStruct((b, nh, HEAD_DIM), F32),
            jax.ShapeDtypeStruct((b, nh, LANE), F32),
        ],
        scratch_shapes=[pltpu.VMEM((nh, HEAD_DIM, HEAD_DIM), F32), pltpu.VMEM((nh, HEAD_DIM), F32),
                        pltpu.VMEM((nh, LANE), F32)],
        compiler_params=pltpu.CompilerParams(
            dimension_semantics=("parallel", "arbitrary"), vmem_limit_bytes=VMEM_LIMIT),
        name="mlstm",
    )(big, big, big, big, small, norm_g[None], c0, n0, m0e)
    return o, c_new, n_new, m_new[:, :, 0]


def _segments(d_model):
    mw = d_model // 2
    nh = mw // HEAD_DIM
    widths = (3 * mw, mw, nh, nh, mw, mw, mw, mw, nh, nh, mw, mw, mw,
              IDX_HEADS * IDX_DIM, IDX_HEADS, IDX_DIM, N_BRANCH * d_model)
    offs = [0]
    for w in widths:
        offs.append(offs[-1] + w)
    names = ('g_qkv', 'g_z', 'g_b', 'g_a', 'm_q', 'm_k', 'm_v', 'm_o', 'm_i', 'm_f',
             'a_q', 'a_k', 'a_v', 'i_q', 'i_w', 'i_k', 'gates')
    return {nm: (offs[i], widths[i]) for i, nm in enumerate(names)}


_BIG = ('g_qkv', 'g_z', 'm_q', 'm_k', 'm_v', 'm_o', 'a_q', 'a_k', 'a_v', 'i_q')
_SMALL = ('i_k', 'g_b', 'g_a', 'm_i', 'm_f', 'i_w')


def _take_cols(a, seg, names):
    return jnp.concatenate([a[..., seg[n][0]:seg[n][0] + seg[n][1]] for n in names], axis=-1)


def _layout(seg, names):
    out, off = {}, 0
    for n in names:
        out[n] = (off, seg[n][1])
        off += seg[n][1]
    return out, off


def _trunk_layer(xs, l, P, rel_bias, sample_ctx, n_prompt, prompt_shape, sample_shape, init_sample):
    d = xs.shape[1]
    mw = d // 2
    nh = mw // HEAD_DIM
    bp, tp = prompt_shape
    bs, ts = sample_shape

    x1 = _ffn(xs, P['norm_g'][l, 0][None], P['ffa_w13'][l], P['ffa_w2'][l])
    g1 = P['norm_g'][l, 1][None]
    lb, ls = P['lay_big'], P['lay_small']
    big, k_rows, v_rows = _inproj(x1, g1, P['w_big'][l], P['b_big'][l][None],
                                  lb['a_k'][0] // mw, lb['a_v'][0] // mw)
    small, kidx = _inproj_small(x1, g1, P['w_small'][l], P['b_small'][l][None],
                                P['idx_ln_g'][l][None], P['idx_ln_b'][l][None])
    att_cols = tuple(lb[n][0] // mw for n in ('a_q', 'a_k', 'a_v', 'i_q'))
    idx_scale = IDX_DIM ** -0.5 * IDX_HEADS ** -0.5

    outs = []
    states = []
    for grp in range(2):
        if grp == 0:
            b, t = bp, tp
            init = (jnp.zeros((bp, nh, HEAD_DIM, HEAD_DIM), F32), jnp.zeros((bp, CONV_W - 1, 3 * mw), F32),
                    jnp.zeros((bp, nh, HEAD_DIM, HEAD_DIM), F32), jnp.zeros((bp, nh, HEAD_DIM), F32),
                    jnp.zeros((bp, nh), F32))
        else:
            b, t = bs, ts
            init = tuple(a[l] for a in init_sample)
        s0, conv0, c0, n0, m0 = init
        row0 = 0 if grp == 0 else n_prompt

        def seg_b(name):
            o, w = lb[name]
            return lax.slice(big, (row0, o), (row0 + b * t, o + w))

        def seg_s(name):
            o, w = ls[name]
            return lax.slice(small, (row0, o), (row0 + b * t, o + w))

        o_a, s_new, conv_new = _gdn(
            big, small, row0, b, t, lb['g_qkv'][0] // (3 * mw), lb['g_z'][0] // mw, ls['g_b'][0], ls['g_a'][0],
            P['gdn_conv_w'][l], P['gdn_a_log'][l], P['gdn_dt_bias'][l], P['gdn_norm_g'][l], s0, conv0)
        o_b, c_new, n_new, m_new = _mlstm(
            big, small, row0, b, t, tuple(lb[nm][0] // mw for nm in ('m_q', 'm_k', 'm_v', 'm_o')),
            ls['m_i'][0], ls['m_f'][0], P['mlstm_norm_g'][l], c0, n0, m0)
        k_att = k_rows[row0 * nh:(row0 + b * t) * nh].reshape(b, t, nh, HEAD_DIM)
        v_att = v_rows[row0 * nh:(row0 + b * t) * nh].reshape(b, t, nh, HEAD_DIM)
        k_idx = kidx[row0:row0 + b * t].reshape(b, t, IDX_DIM)
        w_idx = seg_s('i_w').reshape(b, t, IDX_HEADS) * idx_scale
        if grp == 0:
            o_c = jnp.concatenate([
                _dsa_prompt(big, row0 + bi * t, t, att_cols, w_idx[bi], k_idx[bi].astype(BF16),
                            rel_bias, P['t5_tiles']) for bi in range(b)], axis=0)
        else:
            cache_k, cache_v, cache_idx_k, page_table = sample_ctx
            o_c = _dsa_sample(big, row0, b, t, att_cols,
                              w_idx.transpose(0, 2, 1).reshape(b, IDX_HEADS * t, 1), k_idx,
                              cache_k.reshape(*cache_k.shape[:2], PAGE_SIZE * nh, HEAD_DIM),
                              cache_v.reshape(*cache_v.shape[:2], PAGE_SIZE * nh, HEAD_DIM),
                              cache_idx_k, l, page_table, rel_bias)
        outs.append((o_a.reshape(b * t, mw).astype(BF16), o_b.reshape(b * t, mw).astype(BF16), o_c))
        states.append((k_att, v_att, k_idx, s_new, conv_new, c_new, n_new, m_new))

    oa = jnp.concatenate([outs[0][0], outs[1][0]], axis=0)
    ob = jnp.concatenate([outs[0][1], outs[1][1]], axis=0)
    oc = jnp.concatenate([outs[0][2], outs[1][2]], axis=0)
    x2 = _merge(x1, g1, oa, ob, oc, P['w_gates'][l], P['b_gates'][l][None], P['w_branch'][l], P['w_out'][l])
    x3 = _ffn(x2, P['norm_g'][l, 2][None], P['ffb_w13'][l], P['ffb_w2'][l])
    return x3, states


def _repack_kernel(x_ref, o_ref, *, nb_valid, nb_padded):
    jb = pl.program_id(1) % nb_padded
    o_ref[...] = jnp.where(jb < nb_valid, x_ref[...], 0.0).astype(o_ref.dtype)


def _repack_bf16(w, axis, f, fp):
    nl, a, b = w.shape
    groups = w.shape[axis] // f
    assert f % LANE == 0 and fp % LANE == 0 and groups * f == w.shape[axis]
    nbv, nbp = f // LANE, fp // LANE

    def src_block(j):
        return (j // nbp) * nbv + jnp.minimum(j % nbp, nbv - 1)

    if axis == 2:
        in_spec = pl.BlockSpec((None, a, LANE), lambda l, j: (l, 0, src_block(j)))
        out_spec = pl.BlockSpec((None, a, LANE), lambda l, j: (l, 0, j))
        out_shape = (nl, a, groups * fp)
    else:
        in_spec = pl.BlockSpec((None, LANE, b), lambda l, j: (l, src_block(j), 0))
        out_spec = pl.BlockSpec((None, LANE, b), lambda l, j: (l, j, 0))
        out_shape = (nl, groups * fp, b)
    return pl.pallas_call(
        functools.partial(_repack_kernel, nb_valid=nbv, nb_padded=nbp),
        grid=(nl, groups * nbp),
        in_specs=[in_spec],
        out_specs=out_spec,
        out_shape=jax.ShapeDtypeStruct(out_shape, BF16),
        compiler_params=pltpu.CompilerParams(dimension_semantics=("parallel", "arbitrary")),
        name="repack",
    )(w)


def _prep_ffn(w13, w2, tf=512):
    f = w2.shape[1]
    fp = _round_up(f, tf)
    return _repack_bf16(w13, 2, f, fp), _repack_bf16(w2, 1, f, fp)


def kernel(x_prompt, x_sample, cache_k, cache_v, cache_idx_k, state_gdn, state_gdn_conv, state_mlstm_c,
           state_mlstm_n, state_mlstm_m, page_table, norm_g, final_g, ffa_w13, ffa_w2, ffb_w13, ffb_w2,
           w_in, b_in, gdn_conv_w, gdn_a_log, gdn_dt_bias, gdn_norm_g, mlstm_norm_g, idx_ln_g, idx_ln_b,
           rel_bias, w_branch, w_out):
    bp, tp, d = x_prompt.shape
    bs, ts, _ = x_sample.shape
    depth = w_in.shape[0]
    seg = _segments(d)
    lay_big, _ = _layout(seg, _BIG)
    lay_small, n_small = _layout(seg, _SMALL)
    pad_small = _round_up(n_small, LANE) - n_small

    ffa_w13p, ffa_w2p = _prep_ffn(ffa_w13, ffa_w2)
    ffb_w13p, ffb_w2p = _prep_ffn(ffb_w13, ffb_w2)
    go, gw = seg['gates']
    w_in_b = lax.optimization_barrier(w_in.astype(BF16))
    P = dict(
        norm_g=norm_g,
        ffa_w13=ffa_w13p, ffa_w2=ffa_w2p,
        ffb_w13=ffb_w13p, ffb_w2=ffb_w2p,
        w_big=_take_cols(w_in_b, seg, _BIG), b_big=_take_cols(b_in, seg, _BIG),
        w_small=jnp.pad(_take_cols(w_in, seg, _SMALL), ((0, 0), (0, 0), (0, pad_small))),
        b_small=jnp.pad(_take_cols(b_in, seg, _SMALL), ((0, 0), (0, pad_small))),
        w_gates=w_in_b[:, :, go:go + gw], b_gates=b_in[:, go:go + gw],
        w_branch=w_branch.astype(BF16), w_out=w_out.astype(BF16),
        gdn_conv_w=gdn_conv_w, gdn_a_log=gdn_a_log, gdn_dt_bias=gdn_dt_bias, gdn_norm_g=gdn_norm_g,
        mlstm_norm_g=mlstm_norm_g, idx_ln_g=idx_ln_g, idx_ln_b=idx_ln_b,
        lay_big=lay_big, lay_small=lay_small,
        t5_tiles=_t5_prompt_tiles(rel_bias, DSA_TQ),
    )

    n_prompt = bp * tp
    xs = jnp.concatenate([x_prompt.reshape(n_prompt, d), x_sample.reshape(bs * ts, d)], axis=0)
    init_sample = (state_gdn, state_gdn_conv, state_mlstm_c, state_mlstm_n, state_mlstm_m)
    sample_ctx = (cache_k, cache_v, cache_idx_k, page_table)
    p_st, s_st = [], []
    for l in range(depth):
        xs, (p, s) = _trunk_layer(xs, l, P, rel_bias, sample_ctx, n_prompt, (bp, tp), (bs, ts), init_sample)
        p_st.append(p)
        s_st.append(s)
    y = _final_norm(xs, final_g[None])
    y_prompt = y[:n_prompt].reshape(bp, tp, d)
    y_sample = y[n_prompt:].reshape(bs, ts, d)

    def stacked(states, i):
        return jnp.stack([s[i] for s in states], axis=0)

    return (y_prompt, y_sample,
            *[stacked(p_st, i) for i in range(8)],
            *[stacked(s_st, i) for i in range(8)])
```
